```python
import math
import jax
import jax.numpy as jnp
from jax import lax
import numpy as np

D_MODEL = 1024
BATCH = 8
SEQ = 2048
DEPTH = 2

LRU_WIDTH = D_MODEL
LRU_BLOCKS = 16
LRU_BLOCK = LRU_WIDTH // LRU_BLOCKS
LRU_C = 8.0
CONV_WIDTH = 4
RET_HEADS = 4
RET_QK_DIM = D_MODEL // RET_HEADS
RET_V_DIM = D_MODEL // RET_HEADS
RET_CHUNK = 128
ROPE_BASE = 10000.0
EVEN_IN = 2 * LRU_WIDTH + RET_HEADS * (2 * RET_QK_DIM + 2 * RET_V_DIM)
EVEN_MIX = LRU_WIDTH + RET_HEADS * RET_V_DIM
SSM_INNER = 2 * D_MODEL
SSM_HEAD_DIM = 64
SSM_HEADS = SSM_INNER // SSM_HEAD_DIM
SSM_GROUPS = 4
SSM_STATE = 128
SSM_CHUNK = 128
SSM_CONV_DIM = SSM_INNER + 2 * SSM_GROUPS * SSM_STATE
SSM_IN = SSM_INNER + SSM_CONV_DIM + SSM_HEADS
PEER_HEADS = 8
PEER_KEYS = 128
PEER_EXPERTS = PEER_KEYS * PEER_KEYS
PEER_TOPK = 16
PEER_KEY_DIM = 128
PEER_TOKEN_BLOCK = 128

EPS = 1e-6
N_EVEN = (DEPTH + 1) // 2
N_ODD = DEPTH // 2

kernel_name = 'hybrid_lru_retention_ssd_peer'


def rmsnorm(x, g):
    xf = x.astype(jnp.float32)
    y = xf * lax.rsqrt(jnp.mean(xf * xf, axis=-1, keepdims=True) + EPS)
    return (y * g.astype(jnp.float32)).astype(x.dtype)


def causal_conv(x, w, b):
    y = lax.conv_general_dilated(x, w[:, None, :].astype(x.dtype), window_strides=(1,),
                                 padding=[(CONV_WIDTH - 1, 0)],
                                 dimension_numbers=('NWC', 'WIO', 'NWC'),
                                 feature_group_count=x.shape[-1])
    return y + b.astype(x.dtype)


def rope(x):
    s_, d_ = x.shape[2], x.shape[3]
    half = d_ // 2
    inv = ROPE_BASE ** (-jnp.arange(half, dtype=jnp.float32) * (2.0 / d_))
    ang = jnp.arange(s_, dtype=jnp.float32)[:, None] * inv[None, :]
    cos, sin = jnp.cos(ang), jnp.sin(ang)
    xf = x.astype(jnp.float32)
    x1, x2 = xf[..., :half], xf[..., half:]
    return jnp.concatenate([x1 * cos - x2 * sin, x1 * sin + x2 * cos], axis=-1)


def rg_lru(x, w_a, b_a, w_i, b_i, lam):
    b_, s_, _ = x.shape
    xb = x.reshape(b_, s_, LRU_BLOCKS, LRU_BLOCK)
    r = jax.nn.sigmoid(jnp.einsum('bshi,hij->bshj', xb, w_a).reshape(b_, s_, LRU_WIDTH) + b_a)
    i = jax.nn.sigmoid(jnp.einsum('bshi,hij->bshj', xb, w_i).reshape(b_, s_, LRU_WIDTH) + b_i)
    log_a = (-LRU_C * r.astype(jnp.float32)) * jax.nn.softplus(-lam.astype(jnp.float32))
    a = jnp.exp(log_a)
    u = jnp.sqrt(-jnp.expm1(2.0 * log_a)) * (i * x).astype(jnp.float32)

    def combine(left, right):
        a1, u1 = left
        a2, u2 = right
        return a1 * a2, a2 * u1 + u2

    _, h = lax.associative_scan(combine, (a, u), axis=1)
    return h.astype(x.dtype)


def retention(q, k, v):
    b_, h_, s_, dk = q.shape
    dv = v.shape[-1]
    c = RET_CHUNK
    n = s_ // c
    log_gamma = jnp.log1p(-jnp.exp2(-5.0 - jnp.arange(h_, dtype=jnp.float32)))
    idx = jnp.arange(c, dtype=jnp.float32)
    rel = idx[:, None] - idx[None, :]
    decay_in = jnp.where(rel >= 0, jnp.exp(log_gamma[:, None, None] * jnp.maximum(rel, 0.0)), 0.0)
    qc = q.reshape(b_, h_, n, c, dk) * (dk ** -0.5)
    kc = k.reshape(b_, h_, n, c, dk)
    vc = v.astype(jnp.float32).reshape(b_, h_, n, c, dv)
    scores = jnp.einsum('bhnid,bhnjd->bhnij', qc, kc) * decay_in[None, :, None]
    y_in = jnp.einsum('bhnij,bhnje->bhnie', scores, vc)
    k_decay = jnp.exp(log_gamma[:, None] * (c - 1.0 - idx)[None, :])
    chunk_kv = jnp.einsum('bhnjd,hj,bhnje->nbhde', kc, k_decay, vc)
    chunk_decay = jnp.exp(log_gamma * c)

    def step(state, kv):
        return state * chunk_decay[None, :, None, None] + kv, state

    _, prev = lax.scan(step, jnp.zeros((b_, h_, dk, dv), jnp.float32), chunk_kv)
    q_decay = jnp.exp(log_gamma[:, None] * (idx + 1.0)[None, :])
    y_cross = jnp.einsum('bhnid,hi,nbhde->bhnie', qc, q_decay, prev)
    y = (y_in + y_cross).reshape(b_, h_, s_, dv)
    y = y * lax.rsqrt(jnp.mean(y * y, axis=-1, keepdims=True) + EPS)
    return y.transpose(0, 2, 1, 3).reshape(b_, s_, h_ * dv)


def lru_retention_mixer(h, w_in, lru_conv_w, lru_conv_b, lru_w_a, lru_b_a, lru_w_i, lru_b_i,
                        lru_lambda, w_out):
    b_, s_, _ = h.shape
    proj = h @ w_in
    o1 = LRU_WIDTH
    o2 = o1 + LRU_WIDTH
    o3 = o2 + RET_HEADS * RET_QK_DIM
    o4 = o3 + RET_HEADS * RET_QK_DIM
    o5 = o4 + RET_HEADS * RET_V_DIM
    gate_a, xa = proj[..., :o1], proj[..., o1:o2]
    q, k, v, gate_b = proj[..., o2:o3], proj[..., o3:o4], proj[..., o4:o5], proj[..., o5:]
    xa = causal_conv(xa, lru_conv_w, lru_conv_b)
    ya = jax.nn.gelu(gate_a) * rg_lru(xa, lru_w_a, lru_b_a, lru_w_i, lru_b_i, lru_lambda)
    qh = rope(q.reshape(b_, s_, RET_HEADS, RET_QK_DIM).transpose(0, 2, 1, 3))
    kh = rope(k.reshape(b_, s_, RET_HEADS, RET_QK_DIM).transpose(0, 2, 1, 3))
    vh = v.reshape(b_, s_, RET_HEADS, RET_V_DIM).transpose(0, 2, 1, 3)
    yb = (jax.nn.silu(gate_b.astype(jnp.float32)) * retention(qh, kh, vh)).astype(h.dtype)
    return jnp.concatenate([ya.astype(h.dtype), yb], axis=-1) @ w_out


def ssd(x, dt, a_neg, bm, cm):
    b_, s_, h_, p_ = x.shape
    c = SSM_CHUNK
    n = s_ // c
    g = SSM_GROUPS
    r = h_ // g
    xc = (x * dt[..., None]).reshape(b_, n, c, g, r, p_)
    a = (dt * a_neg).reshape(b_, n, c, g, r)
    a_cum = jnp.cumsum(a, axis=2)
    bc = bm.reshape(b_, n, c, g, SSM_STATE)
    cc = cm.reshape(b_, n, c, g, SSM_STATE)
    mask = jnp.tril(jnp.ones((c, c), dtype=bool))
    seg = a_cum[:, :, :, None] - a_cum[:, :, None, :]
    lmat = jnp.exp(jnp.where(mask[None, None, :, :, None, None], seg, -jnp.inf))
    cb = jnp.einsum('bnlgk,bnsgk->bnlsg', cc, bc)
    y_diag = jnp.einsum('bnlsg,bnlsgr,bnsgrp->bnlgrp', cb, lmat, xc)
    decay_states = jnp.exp(a_cum[:, :, -1:] - a_cum)
    states = jnp.einsum('bnsgk,bnsgr,bnsgrp->nbgrpk', bc, decay_states, xc)
    chunk_decay = jnp.moveaxis(jnp.exp(a_cum[:, :, -1]), 1, 0)

    def step(state, inp):
        dec, st = inp
        return state * dec[..., None, None] + st, state

    init = jnp.zeros((b_, g, r, p_, SSM_STATE), jnp.float32)
    _, prev = lax.scan(step, init, (chunk_decay, states))
    y_off = jnp.einsum('bnlgk,nbgrpk,bnlgr->bnlgrp', cc, prev, jnp.exp(a_cum))
    return (y_diag + y_off).reshape(b_, s_, h_, p_)


def mamba2_mixer(h, w_in, conv_w, conv_b, dt_bias, a_log, d_skip, norm_g, w_out):
    b_, s_, _ = h.shape
    proj = h @ w_in
    z = proj[..., :SSM_INNER]
    xbc = proj[..., SSM_INNER:SSM_INNER + SSM_CONV_DIM]
    dt_raw = proj[..., SSM_INNER + SSM_CONV_DIM:]
    xbc = jax.nn.silu(causal_conv(xbc, conv_w, conv_b)).astype(jnp.float32)
    xs = xbc[..., :SSM_INNER].reshape(b_, s_, SSM_HEADS, SSM_HEAD_DIM)
    bm = xbc[..., SSM_INNER:SSM_INNER + SSM_GROUPS * SSM_STATE].reshape(b_, s_, SSM_GROUPS, SSM_STATE)
    cm = xbc[..., SSM_INNER + SSM_GROUPS * SSM_STATE:].reshape(b_, s_, SSM_GROUPS, SSM_STATE)
    dt = jax.nn.softplus(dt_raw.astype(jnp.float32) + dt_bias.astype(jnp.float32))
    a_neg = -jnp.exp(a_log.astype(jnp.float32))
    y = ssd(xs, dt, a_neg, bm, cm) + d_skip.astype(jnp.float32)[:, None] * xs
    y = y.reshape(b_, s_, SSM_INNER) * jax.nn.silu(z.astype(jnp.float32))
    y = rmsnorm(y, norm_g).astype(h.dtype)
    return y @ w_out


def peer(h, w_q, sub_keys, u_emb, v_emb):
    b_, s_, d_ = h.shape
    t = h.reshape(-1, d_)
    n_tok = t.shape[0]
    q = (t @ w_q).reshape(n_tok, PEER_HEADS, 2, PEER_KEY_DIM // 2)
    scores = jnp.einsum('thpk,hpnk->thpn', q, sub_keys).astype(jnp.float32)
    s_top, i_top = lax.top_k(scores, PEER_TOPK)
    cand = s_top[:, :, 0, :, None] + s_top[:, :, 1, None, :]
    cand_idx = i_top[:, :, 0, :, None] * PEER_KEYS + i_top[:, :, 1, None, :]
    cand = cand.reshape(n_tok, PEER_HEADS, PEER_TOPK * PEER_TOPK)
    cand_idx = cand_idx.reshape(n_tok, PEER_HEADS, PEER_TOPK * PEER_TOPK)
    best, pos = lax.top_k(cand, PEER_TOPK)
    expert = jnp.take_along_axis(cand_idx, pos, axis=-1)
    gates = jax.nn.softmax(best, axis=-1).astype(h.dtype)
    nb = n_tok // PEER_TOKEN_BLOCK

    def block(args):
        tb, eb, gb = args
        u = u_emb[eb]
        act = jax.nn.gelu(jnp.einsum('td,thkd->thk', tb, u)) * gb
        return jnp.einsum('thk,thkd->td', act, v_emb[eb])

    out = lax.map(block, (t.reshape(nb, PEER_TOKEN_BLOCK, d_),
                          expert.reshape(nb, PEER_TOKEN_BLOCK, PEER_HEADS, PEER_TOPK),
                          gates.reshape(nb, PEER_TOKEN_BLOCK, PEER_HEADS, PEER_TOPK)))
    return out.reshape(b_, s_, d_).astype(h.dtype)


def setup_inputs(seed: int = 0) -> dict:
    key = jax.random.key(seed)
    keys = jax.random.split(key, 40)
    cnt = [0]
    f32 = jnp.float32

    def nk():
        k_ = keys[cnt[0]]
        cnt[0] += 1
        return k_

    def nrm(shape, scale):
        return jax.random.normal(nk(), shape, f32) * scale

    def gain(shape):
        return 1.0 + 0.02 * jax.random.normal(nk(), shape, f32)

    x = nrm((BATCH, SEQ, D_MODEL), 1.0)
    mix_norm = gain((DEPTH, D_MODEL))
    ffn_norm = gain((DEPTH, D_MODEL))
    final_norm = gain((D_MODEL,))
    even_w_in = nrm((N_EVEN, D_MODEL, EVEN_IN), D_MODEL ** -0.5)
    lru_conv_w = nrm((N_EVEN, CONV_WIDTH, LRU_WIDTH), CONV_WIDTH ** -0.5)
    lru_conv_b = nrm((N_EVEN, LRU_WIDTH), 0.01)
    lru_w_a = nrm((N_EVEN, LRU_BLOCKS, LRU_BLOCK, LRU_BLOCK), LRU_BLOCK ** -0.5)
    lru_b_a = nrm((N_EVEN, LRU_WIDTH), 0.01)
    lru_w_i = nrm((N_EVEN, LRU_BLOCKS, LRU_BLOCK, LRU_BLOCK), LRU_BLOCK ** -0.5)
    lru_b_i = nrm((N_EVEN, LRU_WIDTH), 0.01)
    a_pow_c = jax.random.uniform(nk(), (N_EVEN, LRU_WIDTH), f32, 0.9, 0.999)
    p = a_pow_c ** (1.0 / LRU_C)
    lru_lambda = jnp.log(p) - jnp.log1p(-p)
    even_w_out = nrm((N_EVEN, EVEN_MIX, D_MODEL), EVEN_MIX ** -0.5)
    ssm_w_in = nrm((N_ODD, D_MODEL, SSM_IN), D_MODEL ** -0.5)
    ssm_conv_w = nrm((N_ODD, CONV_WIDTH, SSM_CONV_DIM), CONV_WIDTH ** -0.5)
    ssm_conv_b = nrm((N_ODD, SSM_CONV_DIM), 0.01)
    dt0 = jnp.exp(jax.random.uniform(nk(), (N_ODD, SSM_HEADS), f32, math.log(1e-3), math.log(1e-1)))
    ssm_dt_bias = dt0 + jnp.log(-jnp.expm1(-dt0))
    ssm_a_log = jnp.log(jax.random.uniform(nk(), (N_ODD, SSM_HEADS), f32, 1.0, 16.0))
    ssm_d = gain((N_ODD, SSM_HEADS))
    ssm_norm = gain((N_ODD, SSM_INNER))
    ssm_w_out = nrm((N_ODD, SSM_INNER, D_MODEL), SSM_INNER ** -0.5)
    peer_w_q = nrm((DEPTH, D_MODEL, PEER_HEADS * PEER_KEY_DIM), D_MODEL ** -0.5)
    peer_sub_keys = nrm((DEPTH, PEER_HEADS, 2, PEER_KEYS, PEER_KEY_DIM // 2), (PEER_KEY_DIM // 2) ** -0.5)
    peer_u = nrm((DEPTH, PEER_EXPERTS, D_MODEL), D_MODEL ** -0.5)
    peer_v = nrm((DEPTH, PEER_EXPERTS, D_MODEL), (PEER_HEADS * PEER_TOPK) ** -0.5)
    return {'x': x, 'mix_norm': mix_norm, 'ffn_norm': ffn_norm, 'final_norm': final_norm,
            'even_w_in': even_w_in, 'lru_conv_w': lru_conv_w, 'lru_conv_b': lru_conv_b,
            'lru_w_a': lru_w_a, 'lru_b_a': lru_b_a, 'lru_w_i': lru_w_i, 'lru_b_i': lru_b_i,
            'lru_lambda': lru_lambda, 'even_w_out': even_w_out,
            'ssm_w_in': ssm_w_in, 'ssm_conv_w': ssm_conv_w, 'ssm_conv_b': ssm_conv_b,
            'ssm_dt_bias': ssm_dt_bias, 'ssm_a_log': ssm_a_log, 'ssm_d': ssm_d,
            'ssm_norm': ssm_norm, 'ssm_w_out': ssm_w_out,
            'peer_w_q': peer_w_q, 'peer_sub_keys': peer_sub_keys, 'peer_u': peer_u, 'peer_v': peer_v}


def reference(x, mix_norm, ffn_norm, final_norm, even_w_in, lru_conv_w, lru_conv_b, lru_w_a,
              lru_b_a, lru_w_i, lru_b_i, lru_lambda, even_w_out, ssm_w_in, ssm_conv_w, ssm_conv_b,
              ssm_dt_bias, ssm_a_log, ssm_d, ssm_norm, ssm_w_out, peer_w_q, peer_sub_keys,
              peer_u, peer_v):
    for layer in range(DEPTH):
        j = layer // 2
        hn = rmsnorm(x, mix_norm[layer])
        if layer % 2 == 0:
            x = x + lru_retention_mixer(hn, even_w_in[j], lru_conv_w[j], lru_conv_b[j], lru_w_a[j],
                                        lru_b_a[j], lru_w_i[j], lru_b_i[j], lru_lambda[j],
                                        even_w_out[j])
        else:
            x = x + mamba2_mixer(hn, ssm_w_in[j], ssm_conv_w[j], ssm_conv_b[j], ssm_dt_bias[j],
                                 ssm_a_log[j], ssm_d[j], ssm_norm[j], ssm_w_out[j])
        hn = rmsnorm(x, ffn_norm[layer])
        x = x + peer(hn, peer_w_q[layer], peer_sub_keys[layer], peer_u[layer], peer_v[layer])
    return rmsnorm(x, final_norm)
```

```python
import functools
import math

import jax
import jax.numpy as jnp
from jax import lax
from jax.experimental import pallas as pl
from jax.experimental.pallas import tpu as pltpu

F32 = jnp.float32
BF16 = jnp.bfloat16

EPS = 1e-6
LANES = 128
SUBLANES = 8
VMEM_LIMIT = 56 * 1024 * 1024

LRU_BLOCK = 64
LRU_C = 8.0
CONV_WIDTH = 4
RET_HEADS = 4
RET_DIM = 256
CHUNK = 128
ROPE_BASE = 10000.0
SSM_HEAD_DIM = 64
SSM_GROUPS = 4
SSM_STATE = 128
PEER_HEADS = 8
PEER_KEYS = 128
PEER_TOPK = 16
NEG_INF = float("-inf")
POS_INF = float("inf")


def _cparams(*sem):
    return pltpu.CompilerParams(dimension_semantics=sem, vmem_limit_bytes=VMEM_LIMIT)


def _softplus(x):
    return jnp.maximum(x, 0.0) + jnp.log1p(jnp.exp(-jnp.abs(x)))


def _sigmoid(x):
    return 1.0 / (1.0 + jnp.exp(-x))


def _silu(x):
    return x * _sigmoid(x)


def _gelu(x):
    c = math.sqrt(2.0 / math.pi)
    return 0.5 * x * (1.0 + jnp.tanh(c * (x + 0.044715 * (x * x * x))))


def _rms(x, g):
    return x * lax.rsqrt(jnp.mean(x * x, axis=-1, keepdims=True) + EPS) * g


def _norm_matmul_kernel(x_ref, g_ref, w_ref, o_ref, hn_ref):
    @pl.when(pl.program_id(1) == 0)
    def _():
        hn_ref[...] = _rms(x_ref[...], g_ref[...]).astype(BF16)

    o_ref[...] = jnp.dot(hn_ref[...], w_ref[...], preferred_element_type=F32).astype(o_ref.dtype)


def _norm_matmul(x, g, w, *, tm, tn, out_dtype=F32):
    t, d = x.shape
    n = w.shape[1]
    return pl.pallas_call(
        _norm_matmul_kernel,
        grid=(t // tm, n // tn),
        in_specs=[pl.BlockSpec((tm, d), lambda i, j: (i, 0)),
                  pl.BlockSpec((1, d), lambda i, j: (0, 0)),
                  pl.BlockSpec((d, tn), lambda i, j: (0, j))],
        out_specs=pl.BlockSpec((tm, tn), lambda i, j: (i, j)),
        out_shape=jax.ShapeDtypeStruct((t, n), out_dtype),
        scratch_shapes=[pltpu.VMEM((tm, d), BF16)],
        compiler_params=_cparams("parallel", "arbitrary"),
        name="norm_matmul",
    )(x, g.reshape(1, d), w)


def _proj_residual_kernel(n_in, x_ref, *refs):
    o_ref = refs[2 * n_in]
    acc = x_ref[...]
    for k in range(n_in):
        acc = acc + jnp.dot(refs[k][...], refs[n_in + k][...], preferred_element_type=F32)
    o_ref[...] = acc


def _proj_residual(x, ys, ws, *, tm, tn):
    t, d = x.shape
    n_in = len(ys)
    in_specs = [pl.BlockSpec((tm, tn), lambda i, j: (i, j))]
    in_specs += [pl.BlockSpec((tm, y.shape[1]), lambda i, j: (i, 0)) for y in ys]
    in_specs += [pl.BlockSpec((w.shape[0], tn), lambda i, j: (0, j)) for w in ws]
    return pl.pallas_call(
        functools.partial(_proj_residual_kernel, n_in),
        grid=(t // tm, d // tn),
        in_specs=in_specs,
        out_specs=pl.BlockSpec((tm, tn), lambda i, j: (i, j)),
        out_shape=jax.ShapeDtypeStruct((t, d), F32),
        compiler_params=_cparams("parallel", "parallel"),
        name="proj_residual",
    )(x, *ys, *ws)


def _lru_kernel(ga_ref, xa_ref, cw_ref, cb_ref, wg_ref, ba_ref, bi_ref, lam_ref, o_ref, a_s, u_s):
    s, ct = xa_ref.shape[1], xa_ref.shape[2]
    x = xa_ref[0]
    row = lax.broadcasted_iota(jnp.int32, (s, ct), 0)
    cw = cw_ref[...]
    xc = x * cw[CONV_WIDTH - 1:CONV_WIDTH, :] + cb_ref[...]
    for k in range(1, CONV_WIDTH):
        xs = jnp.where(row >= k, pltpu.roll(x, k, axis=0), 0.0)
        xc = xc + xs * cw[CONV_WIDTH - 1 - k:CONV_WIDTH - k, :]
    gates = jnp.dot(xc.astype(BF16), wg_ref[0], preferred_element_type=F32)
    r = _sigmoid(gates[:, :ct] + ba_ref[...])
    i = _sigmoid(gates[:, ct:] + bi_ref[...])
    log_a = (-LRU_C * r) * _softplus(-lam_ref[...])
    a_s[...] = jnp.exp(log_a)
    th = jnp.tanh(log_a)
    u_s[...] = jnp.sqrt(-2.0 * th / (1.0 - th)) * (i * xc)

    sub = lax.broadcasted_iota(jnp.int32, (SUBLANES, ct), 0)

    def tile(t, h):
        r0 = pl.multiple_of(t * SUBLANES, SUBLANES)
        a = a_s[pl.ds(r0, SUBLANES), :]
        u = u_s[pl.ds(r0, SUBLANES), :]
        for sh in (1, 2, 4):
            keep = sub >= sh
            u = jnp.where(keep, a * pltpu.roll(u, sh, axis=0) + u, u)
            a = jnp.where(keep, a * pltpu.roll(a, sh, axis=0), a)
        hb = u + a * h
        o_ref[0, pl.ds(r0, SUBLANES), :] = (_gelu(ga_ref[0, pl.ds(r0, SUBLANES), :]) * hb).astype(o_ref.dtype)
        return hb[SUBLANES - 1:SUBLANES, :]

    lax.fori_loop(0, s // SUBLANES, tile, jnp.zeros((1, ct), F32), unroll=4)


def _lru_branch(proj3, conv_w, conv_b, w_gates, b_a, b_i, lam, *, width, ct):
    b, s, _ = proj3.shape
    nct = width // ct
    vec = lambda: pl.BlockSpec((1, ct), lambda bi, j: (0, j))
    return pl.pallas_call(
        _lru_kernel,
        grid=(b, nct),
        in_specs=[pl.BlockSpec((1, s, ct), lambda bi, j: (bi, 0, j)),
                  pl.BlockSpec((1, s, ct), lambda bi, j: (bi, 0, nct + j)),
                  pl.BlockSpec((CONV_WIDTH, ct), lambda bi, j: (0, j)),
                  vec(),
                  pl.BlockSpec((1, ct, 2 * ct), lambda bi, j: (j, 0, 0)),
                  vec(), vec(), vec()],
        out_specs=pl.BlockSpec((1, s, ct), lambda bi, j: (bi, 0, j)),
        out_shape=jax.ShapeDtypeStruct((b, s, width), BF16),
        scratch_shapes=[pltpu.VMEM((s, ct), F32), pltpu.VMEM((s, ct), F32)],
        compiler_params=_cparams("parallel", "parallel"),
        name="rg_lru",
    )(proj3, proj3, conv_w, conv_b.reshape(1, -1), w_gates, b_a.reshape(1, -1), b_i.reshape(1, -1),
      lam.reshape(1, -1))


def _lru_gate_weights(w_a, w_i, ct):
    nb = w_a.shape[0]
    per = ct // LRU_BLOCK

    def bd(w):
        w = w.reshape(nb // per, per, LRU_BLOCK, LRU_BLOCK)
        eye = jnp.eye(per, dtype=w.dtype)
        return jnp.einsum('tpij,pq->tpiqj', w, eye).reshape(nb // per, ct, ct)

    return jnp.concatenate([bd(w_a), bd(w_i)], axis=-1).astype(BF16)


def _retention_kernel(lg_ref, q_ref, k_ref, v_ref, gb_ref, cos_ref, sin_ref, o_ref, q_s, k_s, kd_s, st_s):
    s, dk = q_ref.shape[1], q_ref.shape[2]
    half = dk // 2
    lg = lg_ref[pl.program_id(1)]
    cos = cos_ref[...]
    sin = sin_ref[...]

    def rope(x):
        x1, x2 = x[:, :half], x[:, half:]
        return jnp.concatenate([x1 * cos - x2 * sin, x1 * sin + x2 * cos], axis=-1)

    q_s[...] = (rope(q_ref[0]) * (dk ** -0.5)).astype(BF16)
    k_s[...] = rope(k_ref[0])
    st_s[...] = jnp.zeros_like(st_s)

    ri = lax.broadcasted_iota(jnp.int32, (CHUNK, CHUNK), 0)
    ci = lax.broadcasted_iota(jnp.int32, (CHUNK, CHUNK), 1)
    rel = (ri - ci).astype(F32)
    decay_in = jnp.where(rel >= 0, jnp.exp(lg * jnp.maximum(rel, 0.0)), 0.0)
    rowf = lax.broadcasted_iota(jnp.int32, (CHUNK, dk), 0).astype(F32)
    k_decay = jnp.exp(lg * (CHUNK - 1.0 - rowf))
    q_decay = jnp.exp(lg * (rowf + 1.0))
    chunk_decay = jnp.exp(jnp.full((1, dk), lg * CHUNK, F32))

    def chunk(c, carry):
        r0 = pl.multiple_of(c * CHUNK, CHUNK)
        qc = q_s[pl.ds(r0, CHUNK), :]
        kc = k_s[pl.ds(r0, CHUNK), :]
        vc = v_ref[0, pl.ds(r0, CHUNK), :].astype(BF16)
        scores = lax.dot_general(qc, kc.astype(BF16), (((1,), (1,)), ((), ())), preferred_element_type=F32)
        y = jnp.dot((scores * decay_in).astype(BF16), vc, preferred_element_type=F32)
        state = st_s[...]
        y = y + jnp.dot(qc, state.astype(BF16), preferred_element_type=F32) * q_decay
        kd_s[...] = (kc * k_decay).T.astype(BF16)
        st_s[...] = state * chunk_decay + jnp.dot(kd_s[...], vc, preferred_element_type=F32)
        y = y * lax.rsqrt(jnp.mean(y * y, axis=-1, keepdims=True) + EPS)
        o_ref[0, pl.ds(r0, CHUNK), :] = (_silu(gb_ref[0, pl.ds(r0, CHUNK), :]) * y).astype(o_ref.dtype)
        return carry

    lax.fori_loop(0, s // CHUNK, chunk, 0)


def _retention_branch(proj3, cos, sin, log_gamma, *, col0):
    b, s, _ = proj3.shape
    hd = RET_HEADS
    base = col0 // RET_DIM
    blk = lambda off: pl.BlockSpec((1, s, RET_DIM), lambda bi, h: (bi, 0, base + off + h))
    tab = lambda: pl.BlockSpec((s, RET_DIM // 2), lambda bi, h: (0, 0))
    return pl.pallas_call(
        _retention_kernel,
        grid=(b, hd),
        in_specs=[pl.BlockSpec(memory_space=pltpu.SMEM),
                  blk(0), blk(hd), blk(2 * hd), blk(3 * hd), tab(), tab()],
        out_specs=pl.BlockSpec((1, s, RET_DIM), lambda bi, h: (bi, 0, h)),
        out_shape=jax.ShapeDtypeStruct((b, s, hd * RET_DIM), BF16),
        scratch_shapes=[pltpu.VMEM((s, RET_DIM), BF16), pltpu.VMEM((s, RET_DIM), F32),
                        pltpu.VMEM((RET_DIM, CHUNK), BF16), pltpu.VMEM((RET_DIM, RET_DIM), F32)],
        compiler_params=_cparams("parallel", "parallel"),
        name="retention",
    )(log_gamma, proj3, proj3, proj3, proj3, cos, sin)


def _ssd_kernel(z_ref, xs_ref, bm_ref, cm_ref, dt_ref, cwx_ref, cwb_ref, cwc_ref, cbx_ref, cbb_ref, cbc_ref,
                dtb_ref, aneg_ref, dexp_ref, ng_ref, exp_ref, o_ref, ex_s, eb_s, ec_s, st_s, y_s):
    inner = xs_ref.shape[2]
    gw = inner // SSM_GROUPS
    hpg = gw // SSM_HEAD_DIM

    @pl.when(pl.program_id(1) == 0)
    def _():
        ex_s[0:SUBLANES, :] = jnp.zeros((SUBLANES, ex_s.shape[1]), F32)
        eb_s[0:SUBLANES, :] = jnp.zeros((SUBLANES, eb_s.shape[1]), F32)
        ec_s[0:SUBLANES, :] = jnp.zeros((SUBLANES, ec_s.shape[1]), F32)
        st_s[...] = jnp.zeros_like(st_s)

    def conv_silu(src_ref, ext, cw_ref, cb_ref):
        ext[SUBLANES:SUBLANES + CHUNK, :] = src_ref[0]
        cw = cw_ref[...]
        acc = cb_ref[...] + ext[SUBLANES:SUBLANES + CHUNK, :] * cw[CONV_WIDTH - 1:CONV_WIDTH, :]
        for k in range(1, CONV_WIDTH):
            acc = acc + ext[SUBLANES - k:SUBLANES - k + CHUNK, :] * cw[CONV_WIDTH - 1 - k:CONV_WIDTH - k, :]
        ext[0:SUBLANES, :] = ext[CHUNK:CHUNK + SUBLANES, :]
        return _silu(acc)

    xs = conv_silu(xs_ref, ex_s, cwx_ref, cbx_ref)
    bm = conv_silu(bm_ref, eb_s, cwb_ref, cbb_ref)
    cm = conv_silu(cm_ref, ec_s, cwc_ref, cbc_ref)

    dt = _softplus(dt_ref[0] + dtb_ref[...])
    a = dt * aneg_ref[...]
    row = lax.broadcasted_iota(jnp.int32, (CHUNK, LANES), 0)
    a_cum = a
    sh = 1
    while sh < CHUNK:
        a_cum = a_cum + jnp.where(row >= sh, pltpu.roll(a_cum, sh, axis=0), 0.0)
        sh *= 2
    a_cum_t = a_cum.T

    expand = exp_ref[...]
    dt_e = jnp.dot(dt, expand, preferred_element_type=F32, precision=lax.Precision.HIGHEST)
    ac_e = jnp.dot(a_cum, expand, preferred_element_type=F32, precision=lax.Precision.HIGHEST)
    ac_last = ac_e[CHUNK - 1:CHUNK, :]
    xdt = xs * dt_e
    xds = (xdt * jnp.exp(ac_last - ac_e)).astype(BF16)
    ea_e = jnp.exp(ac_e)
    cdec = jnp.exp(ac_last)
    xdt_b = xdt.astype(BF16)

    li = lax.broadcasted_iota(jnp.int32, (CHUNK, CHUNK), 0)
    si = lax.broadcasted_iota(jnp.int32, (CHUNK, CHUNK), 1)
    causal = li >= si
    lane = lax.broadcasted_iota(jnp.int32, (CHUNK, LANES), 1)
    first_half = lane < SSM_HEAD_DIM

    for g in range(SSM_GROUPS):
        bg = bm[:, g * SSM_STATE:(g + 1) * SSM_STATE]
        cg = cm[:, g * SSM_STATE:(g + 1) * SSM_STATE].astype(BF16)
        bg_b = bg.astype(BF16)
        cb = lax.dot_general(cg, bg_b, (((1,), (1,)), ((), ())), preferred_element_type=F32)
        state = st_s[g]
        y_off = jnp.dot(cg, state.astype(BF16), preferred_element_type=F32) * ea_e[:, g * gw:(g + 1) * gw]
        new_states = jnp.dot(bg.T.astype(BF16), xds[:, g * gw:(g + 1) * gw], preferred_element_type=F32)
        st_s[g] = state * cdec[:, g * gw:(g + 1) * gw] + new_states
        for pr in range(hpg // 2):
            c0 = g * gw + pr * LANES
            xp = xdt_b[:, c0:c0 + LANES]
            ys = []
            for hh in (2 * pr, 2 * pr + 1):
                hd = g * hpg + hh
                seg = a_cum[:, hd:hd + 1] - a_cum_t[hd:hd + 1, :]
                lmat = jnp.exp(jnp.where(causal, seg, -1e30))
                ys.append(jnp.dot((cb * lmat).astype(BF16), xp, preferred_element_type=F32))
            y_s[:, c0:c0 + LANES] = jnp.where(first_half, ys[0], ys[1]) + y_off[:, pr * LANES:(pr + 1) * LANES]

    y = y_s[...] + dexp_ref[...] * xs
    y = y * _silu(z_ref[0])
    o_ref[0] = _rms(y, ng_ref[...]).astype(o_ref.dtype)


def _ssd_core(proj3, conv_w, conv_b, dt_bias, a_log, d_skip, norm_g, *, inner):
    b, s, _ = proj3.shape
    gn = SSM_GROUPS * SSM_STATE
    heads = inner // SSM_HEAD_DIM
    pad = LANES - heads
    dtb = jnp.pad(dt_bias.astype(F32), (0, pad)).reshape(1, LANES)
    aneg = jnp.pad(-jnp.exp(a_log.astype(F32)), (0, pad)).reshape(1, LANES)
    dexp = jnp.repeat(d_skip.astype(F32), SSM_HEAD_DIM).reshape(1, inner)
    expand = (jnp.arange(LANES)[:, None] == (jnp.arange(inner) // SSM_HEAD_DIM)[None, :]).astype(F32)
    cw = conv_w.astype(F32)
    cbias = conv_b.astype(F32).reshape(1, -1)
    full = lambda shape: pl.BlockSpec(shape, lambda bi, c: tuple(0 for _ in shape))
    return pl.pallas_call(
        _ssd_kernel,
        grid=(b, s // CHUNK),
        in_specs=[pl.BlockSpec((1, CHUNK, inner), lambda bi, c: (bi, c, 0)),
                  pl.BlockSpec((1, CHUNK, inner), lambda bi, c: (bi, c, 1)),
                  pl.BlockSpec((1, CHUNK, gn), lambda bi, c: (bi, c, 2 * inner // gn)),
                  pl.BlockSpec((1, CHUNK, gn), lambda bi, c: (bi, c, 2 * inner // gn + 1)),
                  pl.BlockSpec((1, CHUNK, LANES), lambda bi, c: (bi, c, (2 * inner + 2 * gn) // LANES)),
                  full((CONV_WIDTH, inner)), full((CONV_WIDTH, gn)), full((CONV_WIDTH, gn)),
                  full((1, inner)), full((1, gn)), full((1, gn)),
                  full((1, LANES)), full((1, LANES)), full((1, inner)), full((1, inner)),
                  full((LANES, inner))],
        out_specs=pl.BlockSpec((1, CHUNK, inner), lambda bi, c: (bi, c, 0)),
        out_shape=jax.ShapeDtypeStruct((b, s, inner), BF16),
        scratch_shapes=[pltpu.VMEM((CHUNK + SUBLANES, inner), F32),
                        pltpu.VMEM((CHUNK + SUBLANES, gn), F32),
                        pltpu.VMEM((CHUNK + SUBLANES, gn), F32),
                        pltpu.VMEM((SSM_GROUPS, SSM_STATE, inner // SSM_GROUPS), F32),
                        pltpu.VMEM((CHUNK, inner), F32)],
        compiler_params=_cparams("parallel", "arbitrary"),
        name="ssd",
    )(proj3, proj3, proj3, proj3, proj3,
      cw[:, :inner], cw[:, inner:inner + gn], cw[:, inner + gn:],
      cbias[:, :inner], cbias[:, inner:inner + gn], cbias[:, inner + gn:],
      dtb, aneg, dexp, norm_g.astype(F32).reshape(1, inner), expand)


def _norm_t_kernel(x_ref, g_ref, o_ref):
    o_ref[...] = _rms(x_ref[...], g_ref[...]).T.astype(o_ref.dtype)


def _norm_t(x, g, *, tm):
    t, d = x.shape
    return pl.pallas_call(
        _norm_t_kernel,
        grid=(t // tm,),
        in_specs=[pl.BlockSpec((tm, d), lambda i: (i, 0)), pl.BlockSpec((1, d), lambda i: (0, 0))],
        out_specs=pl.BlockSpec((d, tm), lambda i: (0, i)),
        out_shape=jax.ShapeDtypeStruct((d, t), BF16),
        compiler_params=_cparams("parallel"),
        name="peer_norm_t",
    )(x, g.reshape(1, d))


def _top_values(s):
    vals = []
    cur = s
    for _ in range(PEER_TOPK):
        m = jnp.max(cur, axis=0, keepdims=True)
        vals.append(m)
        cur = jnp.where(cur == m, NEG_INF, cur)
    return jnp.concatenate(vals, axis=0)


def _peer_select_kernel(hnt_ref, wqt_ref, keys_ref, thr_ref, w1_ref, s2_ref, w2_ref, qt_s):
    tl = hnt_ref.shape[1]
    kd = keys_ref.shape[2]
    qt_s[...] = jnp.dot(wqt_ref[...], hnt_ref[...], preferred_element_type=F32)
    rk = lax.broadcasted_iota(jnp.int32, (PEER_TOPK, LANES), 0).astype(F32)

    def head(h, carry):
        for c in range(tl // LANES):
            ls = slice(c * LANES, (c + 1) * LANES)
            r1 = pl.multiple_of(h * (2 * kd), 2 * kd)
            q1 = qt_s[pl.ds(r1, kd), ls].astype(BF16)
            q2 = qt_s[pl.ds(r1 + kd, kd), ls].astype(BF16)
            s1 = jnp.dot(keys_ref[2 * h], q1, preferred_element_type=F32)
            s2 = jnp.dot(keys_ref[2 * h + 1], q2, preferred_element_type=F32)
            a = _top_values(s1)
            b = _top_values(s2)
            a0, b0 = a[0:1], b[0:1]
            best0 = a0 + b0
            cnt = jnp.zeros((PEER_TOPK, LANES), F32)
            bcur = jnp.broadcast_to(b0, (PEER_TOPK, LANES))
            cur = a + bcur
            thr_a = jnp.full((PEER_TOPK, LANES), POS_INF, F32)
            z = jnp.zeros((1, LANES), F32)
            for _ in range(PEER_TOPK):
                m = jnp.max(cur, axis=0, keepdims=True)
                idx = jnp.min(jnp.where(cur == m, rk, float(PEER_TOPK)), axis=0, keepdims=True)
                sel = rk == idx
                z = z + jnp.exp(m - best0)
                thr_a = jnp.where(sel, bcur, thr_a)
                cnt = jnp.where(sel, cnt + 1.0, cnt)
                csel = jnp.max(jnp.where(sel, cnt, 0.0), axis=0, keepdims=True)
                nb = jnp.max(jnp.where(rk == csel, b, NEG_INF), axis=0, keepdims=True)
                bcur = jnp.where(sel, nb, bcur)
                cur = jnp.where(sel, a + nb, cur)
            thr = jnp.full(s1.shape, POS_INF, F32)
            for r in range(PEER_TOPK):
                thr = jnp.where(s1 == a[r:r + 1], thr_a[r:r + 1], thr)
            thr_ref[h, :, ls] = thr
            w1_ref[h, :, ls] = jnp.exp(s1 - a0)
            s2_ref[h, :, ls] = s2
            w2_ref[h, :, ls] = jnp.exp(s2 - b0) * (1.0 / z)
        return carry

    lax.fori_loop(0, PEER_HEADS, head, 0)


def _peer_select(hnt, wqt, keys, *, tl):
    d, t = hnt.shape
    nq = wqt.shape[0]
    nk = keys.shape[1]
    out = jax.ShapeDtypeStruct((PEER_HEADS, nk, t), F32)
    ospec = lambda: pl.BlockSpec((PEER_HEADS, nk, tl), lambda i: (0, 0, i))
    return pl.pallas_call(
        _peer_select_kernel,
        grid=(t // tl,),
        in_specs=[pl.BlockSpec((d, tl), lambda i: (0, i)),
                  pl.BlockSpec((nq, d), lambda i: (0, 0)),
                  pl.BlockSpec(keys.shape, lambda i: (0, 0, 0))],
        out_specs=[ospec(), ospec(), ospec(), ospec()],
        out_shape=[out, out, out, out],
        scratch_shapes=[pltpu.VMEM((nq, tl), F32)],
        compiler_params=_cparams("parallel"),
        name="peer_select",
    )(hnt, wqt, keys)


def _peer_dense_kernel(sub, hnt_ref, u_ref, vt_ref, thr_ref, w1_ref, s2_ref, w2_ref, x_ref, o_ref, acc_s):
    tn = u_ref.shape[0]
    tm = hnt_ref.shape[1]
    j = pl.program_id(1)

    @pl.when(j == 0)
    def _():
        acc_s[...] = jnp.zeros_like(acc_s)

    per = sub // PEER_KEYS

    for k in range(tn // sub):
        e0 = k * sub
        ht = jnp.dot(u_ref[e0:e0 + sub, :], hnt_ref[...], preferred_element_type=F32)
        parts = []
        for il in range(per):
            i1 = k * per + il
            cols = []
            for c in range(tm // LANES):
                ls = slice(c * LANES, (c + 1) * LANES)
                g = jnp.zeros((PEER_KEYS, LANES), F32)
                for h in range(PEER_HEADS):
                    thr = thr_ref[h, i1:i1 + 1, ls]
                    w1 = w1_ref[h, i1:i1 + 1, ls]
                    g = g + jnp.where(s2_ref[h, :, ls] >= thr, w2_ref[h, :, ls], 0.0) * w1
                cols.append(g)
            parts.append(jnp.concatenate(cols, axis=1))
        gate = jnp.concatenate(parts, axis=0)
        act = (_gelu(ht) * gate).astype(BF16)
        acc_s[...] += jnp.dot(vt_ref[:, e0:e0 + sub], act, preferred_element_type=F32)

    @pl.when(j == pl.num_programs(1) - 1)
    def _():
        o_ref[...] = x_ref[...] + acc_s[...].T


def _peer_dense(x, hnt, u, vt, thr, w1, s2, w2, *, tm, tn, sub):
    t, d = x.shape
    e = u.shape[0]
    nk = s2.shape[1]
    rows = tn // PEER_KEYS
    return pl.pallas_call(
        functools.partial(_peer_dense_kernel, sub),
        grid=(t // tm, e // tn),
        in_specs=[pl.BlockSpec((d, tm), lambda i, j: (0, i)),
                  pl.BlockSpec((tn, d), lambda i, j: (j, 0)),
                  pl.BlockSpec((d, tn), lambda i, j: (0, j)),
                  pl.BlockSpec((PEER_HEADS, rows, tm), lambda i, j: (0, j, i)),
                  pl.BlockSpec((PEER_HEADS, rows, tm), lambda i, j: (0, j, i)),
                  pl.BlockSpec((PEER_HEADS, nk, tm), lambda i, j: (0, 0, i)),
                  pl.BlockSpec((PEER_HEADS, nk, tm), lambda i, j: (0, 0, i)),
                  pl.BlockSpec((tm, d), lambda i, j: (i, 0))],
        out_specs=pl.BlockSpec((tm, d), lambda i, j: (i, 0)),
        out_shape=jax.ShapeDtypeStruct((t, d), F32),
        scratch_shapes=[pltpu.VMEM((d, tm), F32)],
        compiler_params=_cparams("parallel", "arbitrary"),
        name="peer_dense",
    )(hnt, u, vt, thr, w1, s2, w2, x)


def _peer_layer(x, g, w_q, sub_keys, u_emb, v_emb, *, tm_norm, tl, tm, tn, sub):
    hd, two, nk, kd = sub_keys.shape
    hnt = _norm_t(x, g, tm=tm_norm)
    thr, w1, s2, w2 = _peer_select(hnt, w_q.T.astype(BF16), sub_keys.reshape(hd * two, nk, kd).astype(BF16), tl=tl)
    return _peer_dense(x, hnt, u_emb.astype(BF16), v_emb.T.astype(BF16), thr, w1, s2, w2, tm=tm, tn=tn, sub=sub)


def _final_norm_kernel(x_ref, g_ref, o_ref):
    o_ref[...] = _rms(x_ref[...], g_ref[...])


def _final_norm(x, g, *, tm):
    t, d = x.shape
    return pl.pallas_call(
        _final_norm_kernel,
        grid=(t // tm,),
        in_specs=[pl.BlockSpec((tm, d), lambda i: (i, 0)), pl.BlockSpec((1, d), lambda i: (0, 0))],
        out_specs=pl.BlockSpec((tm, d), lambda i: (i, 0)),
        out_shape=jax.ShapeDtypeStruct((t, d), F32),
        compiler_params=_cparams("parallel"),
        name="final_norm",
    )(x, g.reshape(1, d))


def _rope_tables(s, dk):
    half = dk // 2
    inv = ROPE_BASE ** (-jnp.arange(half, dtype=F32) * (2.0 / dk))
    ang = jnp.arange(s, dtype=F32)[:, None] * inv[None, :]
    return jnp.cos(ang), jnp.sin(ang)


def _tile(n, want):
    if n <= want:
        return n
    for cand in range(want, 0, -LANES):
        if n % cand == 0:
            return cand
    return n


def _even_layer(x, b, s, g, w_in, conv_w, conv_b, w_a, b_a, w_i, b_i, lam, w_out):
    t, d = x.shape
    width = w_a.shape[0] * LRU_BLOCK
    n_in = w_in.shape[1]
    proj = _norm_matmul(x, g, w_in.astype(BF16), tm=_tile(t, 512), tn=_tile(n_in, 512))
    proj3 = proj.reshape(b, s, n_in)
    ct = 256
    ya = _lru_branch(proj3, conv_w.astype(F32), conv_b.astype(F32), _lru_gate_weights(w_a, w_i, ct),
                     b_a.astype(F32), b_i.astype(F32), lam.astype(F32), width=width, ct=ct)
    cos, sin = _rope_tables(s, RET_DIM)
    log_gamma = jnp.log1p(-jnp.exp2(-5.0 - jnp.arange(RET_HEADS, dtype=F32)))
    yb = _retention_branch(proj3, cos, sin, log_gamma, col0=2 * width)
    wo = w_out.astype(BF16)
    return _proj_residual(x, [ya.reshape(t, width), yb.reshape(t, -1)], [wo[:width], wo[width:]],
                          tm=_tile(t, 512), tn=_tile(d, 512))


def _odd_layer(x, b, s, g, w_in, conv_w, conv_b, dt_bias, a_log, d_skip, norm_g, w_out):
    t, d = x.shape
    inner = w_out.shape[0]
    n_in = w_in.shape[1]
    tn = 512
    n_pad = -(-n_in // tn) * tn
    w_in_p = jnp.pad(w_in.astype(BF16), ((0, 0), (0, n_pad - n_in)))
    proj = _norm_matmul(x, g, w_in_p, tm=_tile(t, 512), tn=tn)
    y = _ssd_core(proj.reshape(b, s, n_pad), conv_w, conv_b, dt_bias, a_log, d_skip, norm_g, inner=inner)
    return _proj_residual(x, [y.reshape(t, inner)], [w_out.astype(BF16)], tm=_tile(t, 512), tn=_tile(d, 512))


def kernel(x, mix_norm, ffn_norm, final_norm, even_w_in, lru_conv_w, lru_conv_b, lru_w_a, lru_b_a, lru_w_i,
           lru_b_i, lru_lambda, even_w_out, ssm_w_in, ssm_conv_w, ssm_conv_b, ssm_dt_bias, ssm_a_log, ssm_d,
           ssm_norm, ssm_w_out, peer_w_q, peer_sub_keys, peer_u, peer_v):
    b, s, d = x.shape
    t = b * s
    depth = mix_norm.shape[0]
    h = x.reshape(t, d).astype(F32)
    for layer in range(depth):
        j = layer // 2
        if layer % 2 == 0:
            h = _even_layer(h, b, s, mix_norm[layer], even_w_in[j], lru_conv_w[j], lru_conv_b[j], lru_w_a[j],
                            lru_b_a[j], lru_w_i[j], lru_b_i[j], lru_lambda[j], even_w_out[j])
        else:
            h = _odd_layer(h, b, s, mix_norm[layer], ssm_w_in[j], ssm_conv_w[j], ssm_conv_b[j], ssm_dt_bias[j],
                           ssm_a_log[j], ssm_d[j], ssm_norm[j], ssm_w_out[j])
        h = _peer_layer(h, ffn_norm[layer], peer_w_q[layer], peer_sub_keys[layer], peer_u[layer], peer_v[layer],
                        tm_norm=_tile(t, 512), tl=_tile(t, 512), tm=_tile(t, 512), tn=1024, sub=256)
    return _final_norm(h, final_norm, tm=_tile(t, 512)).reshape(b, s, d).astype(x.dtype)
```

```python
import functools
import math

import jax
import jax.numpy as jnp
from jax import lax
from jax.experimental import pallas as pl
from jax.experimental.pallas import tpu as pltpu

F32 = jnp.float32
BF16 = jnp.bfloat16

EPS = 1e-6
LANES = 128
SUBLANES = 8
VMEM_LIMIT = 56 * 1024 * 1024

LRU_BLOCK = 64
LRU_C = 8.0
CONV_WIDTH = 4
RET_HEADS = 4
RET_DIM = 256
CHUNK = 128
ROPE_BASE = 10000.0
SSM_HEAD_DIM = 64
SSM_GROUPS = 4
SSM_STATE = 128
PEER_HEADS = 8
PEER_KEYS = 128
PEER_TOPK = 16
NEG_INF = float("-inf")
POS_INF = float("inf")


def _cparams(*sem):
    return pltpu.CompilerParams(dimension_semantics=sem, vmem_limit_bytes=VMEM_LIMIT)


def _softplus(x):
    return jnp.maximum(x, 0.0) + jnp.log1p(jnp.exp(-jnp.abs(x)))


def _sigmoid(x):
    return 1.0 / (1.0 + jnp.exp(-x))


def _silu(x):
    return x * _sigmoid(x)


def _gelu(x):
    c = math.sqrt(2.0 / math.pi)
    hx = 0.5 * x
    return hx + hx * jnp.tanh(x * (c + (c * 0.044715) * (x * x)))


def _pack_rows(x):
    return pltpu.bitcast(x.astype(BF16), jnp.uint32)


def _pack_rows_xla(w):
    r, c = w.shape
    bits = lax.bitcast_convert_type(w.astype(BF16), jnp.uint16).astype(jnp.uint32).reshape(r // 2, 2, c)
    return bits[:, 0, :] | (bits[:, 1, :] << 16)


def _unpack_rows(x):
    return pltpu.bitcast(x, BF16)


def _rms(x, g):
    return x * lax.rsqrt(jnp.mean(x * x, axis=-1, keepdims=True) + EPS) * g


def _norm_matmul_kernel(x_ref, g_ref, w_ref, o_ref, hn_ref):
    @pl.when(pl.program_id(1) == 0)
    def _():
        hn_ref[...] = _rms(x_ref[...], g_ref[...]).astype(BF16)

    o_ref[...] = jnp.dot(hn_ref[...], w_ref[...], preferred_element_type=F32).astype(o_ref.dtype)


def _norm_matmul(x, g, w, *, tm, tn, out_dtype=F32):
    t, d = x.shape
    n = w.shape[1]
    return pl.pallas_call(
        _norm_matmul_kernel,
        grid=(t // tm, n // tn),
        in_specs=[pl.BlockSpec((tm, d), lambda i, j: (i, 0)),
                  pl.BlockSpec((1, d), lambda i, j: (0, 0)),
                  pl.BlockSpec((d, tn), lambda i, j: (0, j))],
        out_specs=pl.BlockSpec((tm, tn), lambda i, j: (i, j)),
        out_shape=jax.ShapeDtypeStruct((t, n), out_dtype),
        scratch_shapes=[pltpu.VMEM((tm, d), BF16)],
        compiler_params=_cparams("parallel", "arbitrary"),
        name="norm_matmul",
    )(x, g.reshape(1, d), w)


def _proj_residual_kernel(n_in, x_ref, *refs):
    o_ref = refs[2 * n_in]
    acc = x_ref[...]
    for k in range(n_in):
        acc = acc + jnp.dot(refs[k][...], refs[n_in + k][...], preferred_element_type=F32)
    o_ref[...] = acc


def _proj_residual(x, ys, ws, *, tm, tn):
    t, d = x.shape
    n_in = len(ys)
    in_specs = [pl.BlockSpec((tm, tn), lambda i, j: (i, j))]
    in_specs += [pl.BlockSpec((tm, y.shape[1]), lambda i, j: (i, 0)) for y in ys]
    in_specs += [pl.BlockSpec((w.shape[0], tn), lambda i, j: (0, j)) for w in ws]
    return pl.pallas_call(
        functools.partial(_proj_residual_kernel, n_in),
        grid=(t // tm, d // tn),
        in_specs=in_specs,
        out_specs=pl.BlockSpec((tm, tn), lambda i, j: (i, j)),
        out_shape=jax.ShapeDtypeStruct((t, d), F32),
        compiler_params=_cparams("parallel", "parallel"),
        name="proj_residual",
    )(x, *ys, *ws)


def _lru_kernel(ga_ref, xa_ref, cw_ref, cb_ref, wg_ref, ba_ref, bi_ref, lam_ref, o_ref, a_s, u_s):
    s, ct = xa_ref.shape[1], xa_ref.shape[2]
    x = xa_ref[0]
    row = lax.broadcasted_iota(jnp.int32, (s, ct), 0)
    cw = cw_ref[...]
    xc = x * cw[CONV_WIDTH - 1:CONV_WIDTH, :] + cb_ref[...]
    for k in range(1, CONV_WIDTH):
        xs = jnp.where(row >= k, pltpu.roll(x, k, axis=0), 0.0)
        xc = xc + xs * cw[CONV_WIDTH - 1 - k:CONV_WIDTH - k, :]
    gates = jnp.dot(xc.astype(BF16), wg_ref[0], preferred_element_type=F32)
    r = _sigmoid(gates[:, :ct] + ba_ref[...])
    i = _sigmoid(gates[:, ct:] + bi_ref[...])
    log_a = (-LRU_C * r) * _softplus(-lam_ref[...])
    a_s[...] = jnp.exp(log_a)
    th = jnp.tanh(log_a)
    u_s[...] = jnp.sqrt(-2.0 * th / (1.0 - th)) * (i * xc)

    sub = lax.broadcasted_iota(jnp.int32, (SUBLANES, ct), 0)

    def tile(t, h):
        r0 = pl.multiple_of(t * SUBLANES, SUBLANES)
        a = a_s[pl.ds(r0, SUBLANES), :]
        u = u_s[pl.ds(r0, SUBLANES), :]
        for sh in (1, 2, 4):
            keep = sub >= sh
            u = jnp.where(keep, a * pltpu.roll(u, sh, axis=0) + u, u)
            a = jnp.where(keep, a * pltpu.roll(a, sh, axis=0), a)
        hb = u + a * h
        o_ref[0, pl.ds(r0, SUBLANES), :] = (_gelu(ga_ref[0, pl.ds(r0, SUBLANES), :]) * hb).astype(o_ref.dtype)
        return hb[SUBLANES - 1:SUBLANES, :]

    lax.fori_loop(0, s // SUBLANES, tile, jnp.zeros((1, ct), F32), unroll=4)


def _lru_branch(proj3, conv_w, conv_b, w_gates, b_a, b_i, lam, *, width, ct):
    b, s, _ = proj3.shape
    nct = width // ct
    vec = lambda: pl.BlockSpec((1, ct), lambda bi, j: (0, j))
    return pl.pallas_call(
        _lru_kernel,
        grid=(b, nct),
        in_specs=[pl.BlockSpec((1, s, ct), lambda bi, j: (bi, 0, j)),
                  pl.BlockSpec((1, s, ct), lambda bi, j: (bi, 0, nct + j)),
                  pl.BlockSpec((CONV_WIDTH, ct), lambda bi, j: (0, j)),
                  vec(),
                  pl.BlockSpec((1, ct, 2 * ct), lambda bi, j: (j, 0, 0)),
                  vec(), vec(), vec()],
        out_specs=pl.BlockSpec((1, s, ct), lambda bi, j: (bi, 0, j)),
        out_shape=jax.ShapeDtypeStruct((b, s, width), BF16),
        scratch_shapes=[pltpu.VMEM((s, ct), F32), pltpu.VMEM((s, ct), F32)],
        compiler_params=_cparams("parallel", "parallel"),
        name="rg_lru",
    )(proj3, proj3, conv_w, conv_b.reshape(1, -1), w_gates, b_a.reshape(1, -1), b_i.reshape(1, -1),
      lam.reshape(1, -1))


def _lru_gate_weights(w_a, w_i, ct):
    nb = w_a.shape[0]
    per = ct // LRU_BLOCK

    def bd(w):
        w = w.reshape(nb // per, per, LRU_BLOCK, LRU_BLOCK)
        eye = jnp.eye(per, dtype=w.dtype)
        return jnp.einsum('tpij,pq->tpiqj', w, eye).reshape(nb // per, ct, ct)

    return jnp.concatenate([bd(w_a), bd(w_i)], axis=-1).astype(BF16)


def _retention_kernel(lg_ref, q_ref, k_ref, v_ref, gb_ref, cos_ref, sin_ref, o_ref, q_s, k_s, kd_s, st_s):
    s, dk = q_ref.shape[1], q_ref.shape[2]
    half = dk // 2
    lg = lg_ref[pl.program_id(1)]
    cos = cos_ref[...]
    sin = sin_ref[...]

    def rope(x):
        x1, x2 = x[:, :half], x[:, half:]
        return jnp.concatenate([x1 * cos - x2 * sin, x1 * sin + x2 * cos], axis=-1)

    q_s[...] = (rope(q_ref[0]) * (dk ** -0.5)).astype(BF16)
    k_s[...] = rope(k_ref[0])
    st_s[...] = jnp.zeros_like(st_s)

    ri = lax.broadcasted_iota(jnp.int32, (CHUNK, CHUNK), 0)
    ci = lax.broadcasted_iota(jnp.int32, (CHUNK, CHUNK), 1)
    rel = (ri - ci).astype(F32)
    decay_in = jnp.where(rel >= 0, jnp.exp(lg * jnp.maximum(rel, 0.0)), 0.0)
    rowf = lax.broadcasted_iota(jnp.int32, (CHUNK, dk), 0).astype(F32)
    k_decay = jnp.exp(lg * (CHUNK - 1.0 - rowf))
    q_decay = jnp.exp(lg * (rowf + 1.0))
    chunk_decay = jnp.exp(jnp.full((1, dk), lg * CHUNK, F32))

    def chunk(c, carry):
        r0 = pl.multiple_of(c * CHUNK, CHUNK)
        qc = q_s[pl.ds(r0, CHUNK), :]
        kc = k_s[pl.ds(r0, CHUNK), :]
        vc = v_ref[0, pl.ds(r0, CHUNK), :].astype(BF16)
        scores = lax.dot_general(qc, kc.astype(BF16), (((1,), (1,)), ((), ())), preferred_element_type=F32)
        y = jnp.dot((scores * decay_in).astype(BF16), vc, preferred_element_type=F32)
        state = st_s[...]
        y = y + jnp.dot(qc, state.astype(BF16), preferred_element_type=F32) * q_decay
        kd_s[...] = (kc * k_decay).T.astype(BF16)
        st_s[...] = state * chunk_decay + jnp.dot(kd_s[...], vc, preferred_element_type=F32)
        y = y * lax.rsqrt(jnp.mean(y * y, axis=-1, keepdims=True) + EPS)
        o_ref[0, pl.ds(r0, CHUNK), :] = (_silu(gb_ref[0, pl.ds(r0, CHUNK), :]) * y).astype(o_ref.dtype)
        return carry

    lax.fori_loop(0, s // CHUNK, chunk, 0)


def _retention_branch(proj3, cos, sin, log_gamma, *, col0):
    b, s, _ = proj3.shape
    hd = RET_HEADS
    base = col0 // RET_DIM
    blk = lambda off: pl.BlockSpec((1, s, RET_DIM), lambda bi, h: (bi, 0, base + off + h))
    tab = lambda: pl.BlockSpec((s, RET_DIM // 2), lambda bi, h: (0, 0))
    return pl.pallas_call(
        _retention_kernel,
        grid=(b, hd),
        in_specs=[pl.BlockSpec(memory_space=pltpu.SMEM),
                  blk(0), blk(hd), blk(2 * hd), blk(3 * hd), tab(), tab()],
        out_specs=pl.BlockSpec((1, s, RET_DIM), lambda bi, h: (bi, 0, h)),
        out_shape=jax.ShapeDtypeStruct((b, s, hd * RET_DIM), BF16),
        scratch_shapes=[pltpu.VMEM((s, RET_DIM), BF16), pltpu.VMEM((s, RET_DIM), F32),
                        pltpu.VMEM((RET_DIM, CHUNK), BF16), pltpu.VMEM((RET_DIM, RET_DIM), F32)],
        compiler_params=_cparams("parallel", "parallel"),
        name="retention",
    )(log_gamma, proj3, proj3, proj3, proj3, cos, sin)


def _ssd_kernel(z_ref, xs_ref, bm_ref, cm_ref, dt_ref, cwx_ref, cwb_ref, cwc_ref, cbx_ref, cbb_ref, cbc_ref,
                dtb_ref, aneg_ref, dexp_ref, ng_ref, exp_ref, o_ref, ex_s, eb_s, ec_s, st_s, y_s):
    inner = xs_ref.shape[2]
    gw = inner // SSM_GROUPS
    hpg = gw // SSM_HEAD_DIM

    @pl.when(pl.program_id(1) == 0)
    def _():
        ex_s[0:SUBLANES, :] = jnp.zeros((SUBLANES, ex_s.shape[1]), F32)
        eb_s[0:SUBLANES, :] = jnp.zeros((SUBLANES, eb_s.shape[1]), F32)
        ec_s[0:SUBLANES, :] = jnp.zeros((SUBLANES, ec_s.shape[1]), F32)
        st_s[...] = jnp.zeros_like(st_s)

    def conv_silu(src_ref, ext, cw_ref, cb_ref):
        ext[SUBLANES:SUBLANES + CHUNK, :] = src_ref[0]
        cw = cw_ref[...]
        acc = cb_ref[...] + ext[SUBLANES:SUBLANES + CHUNK, :] * cw[CONV_WIDTH - 1:CONV_WIDTH, :]
        for k in range(1, CONV_WIDTH):
            acc = acc + ext[SUBLANES - k:SUBLANES - k + CHUNK, :] * cw[CONV_WIDTH - 1 - k:CONV_WIDTH - k, :]
        ext[0:SUBLANES, :] = ext[CHUNK:CHUNK + SUBLANES, :]
        return _silu(acc)

    xs = conv_silu(xs_ref, ex_s, cwx_ref, cbx_ref)
    bm = conv_silu(bm_ref, eb_s, cwb_ref, cbb_ref)
    cm = conv_silu(cm_ref, ec_s, cwc_ref, cbc_ref)

    dt = _softplus(dt_ref[0] + dtb_ref[...])
    a = dt * aneg_ref[...]
    row = lax.broadcasted_iota(jnp.int32, (CHUNK, LANES), 0)
    a_cum = a
    sh = 1
    while sh < CHUNK:
        a_cum = a_cum + jnp.where(row >= sh, pltpu.roll(a_cum, sh, axis=0), 0.0)
        sh *= 2
    a_cum_t = a_cum.T

    expand = exp_ref[...]
    dt_e = jnp.dot(dt, expand, preferred_element_type=F32, precision=lax.Precision.HIGHEST)
    ac_e = jnp.dot(a_cum, expand, preferred_element_type=F32, precision=lax.Precision.HIGHEST)
    ac_last = ac_e[CHUNK - 1:CHUNK, :]
    xdt = xs * dt_e
    xds = (xdt * jnp.exp(ac_last - ac_e)).astype(BF16)
    ea_e = jnp.exp(ac_e)
    cdec = jnp.exp(ac_last)
    xdt_b = xdt.astype(BF16)

    li = lax.broadcasted_iota(jnp.int32, (CHUNK, CHUNK), 0)
    si = lax.broadcasted_iota(jnp.int32, (CHUNK, CHUNK), 1)
    causal = li >= si
    lane = lax.broadcasted_iota(jnp.int32, (CHUNK, LANES), 1)
    first_half = lane < SSM_HEAD_DIM

    for g in range(SSM_GROUPS):
        bg = bm[:, g * SSM_STATE:(g + 1) * SSM_STATE]
        cg = cm[:, g * SSM_STATE:(g + 1) * SSM_STATE].astype(BF16)
        bg_b = bg.astype(BF16)
        cb = lax.dot_general(cg, bg_b, (((1,), (1,)), ((), ())), preferred_element_type=F32)
        state = st_s[g]
        y_off = jnp.dot(cg, state.astype(BF16), preferred_element_type=F32) * ea_e[:, g * gw:(g + 1) * gw]
        new_states = jnp.dot(bg.T.astype(BF16), xds[:, g * gw:(g + 1) * gw], preferred_element_type=F32)
        st_s[g] = state * cdec[:, g * gw:(g + 1) * gw] + new_states
        for pr in range(hpg // 2):
            c0 = g * gw + pr * LANES
            xp = xdt_b[:, c0:c0 + LANES]
            ys = []
            for hh in (2 * pr, 2 * pr + 1):
                hd = g * hpg + hh
                seg = a_cum[:, hd:hd + 1] - a_cum_t[hd:hd + 1, :]
                lmat = jnp.exp(jnp.where(causal, seg, -1e30))
                ys.append(jnp.dot((cb * lmat).astype(BF16), xp, preferred_element_type=F32))
            y_s[:, c0:c0 + LANES] = jnp.where(first_half, ys[0], ys[1]) + y_off[:, pr * LANES:(pr + 1) * LANES]

    y = y_s[...] + dexp_ref[...] * xs
    y = y * _silu(z_ref[0])
    o_ref[0] = _rms(y, ng_ref[...]).astype(o_ref.dtype)


def _ssd_core(proj3, conv_w, conv_b, dt_bias, a_log, d_skip, norm_g, *, inner):
    b, s, _ = proj3.shape
    gn = SSM_GROUPS * SSM_STATE
    heads = inner // SSM_HEAD_DIM
    pad = LANES - heads
    dtb = jnp.pad(dt_bias.astype(F32), (0, pad)).reshape(1, LANES)
    aneg = jnp.pad(-jnp.exp(a_log.astype(F32)), (0, pad)).reshape(1, LANES)
    dexp = jnp.repeat(d_skip.astype(F32), SSM_HEAD_DIM).reshape(1, inner)
    expand = (jnp.arange(LANES)[:, None] == (jnp.arange(inner) // SSM_HEAD_DIM)[None, :]).astype(F32)
    cw = conv_w.astype(F32)
    cbias = conv_b.astype(F32).reshape(1, -1)
    full = lambda shape: pl.BlockSpec(shape, lambda bi, c: tuple(0 for _ in shape))
    return pl.pallas_call(
        _ssd_kernel,
        grid=(b, s // CHUNK),
        in_specs=[pl.BlockSpec((1, CHUNK, inner), lambda bi, c: (bi, c, 0)),
                  pl.BlockSpec((1, CHUNK, inner), lambda bi, c: (bi, c, 1)),
                  pl.BlockSpec((1, CHUNK, gn), lambda bi, c: (bi, c, 2 * inner // gn)),
                  pl.BlockSpec((1, CHUNK, gn), lambda bi, c: (bi, c, 2 * inner // gn + 1)),
                  pl.BlockSpec((1, CHUNK, LANES), lambda bi, c: (bi, c, (2 * inner + 2 * gn) // LANES)),
                  full((CONV_WIDTH, inner)), full((CONV_WIDTH, gn)), full((CONV_WIDTH, gn)),
                  full((1, inner)), full((1, gn)), full((1, gn)),
                  full((1, LANES)), full((1, LANES)), full((1, inner)), full((1, inner)),
                  full((LANES, inner))],
        out_specs=pl.BlockSpec((1, CHUNK, inner), lambda bi, c: (bi, c, 0)),
        out_shape=jax.ShapeDtypeStruct((b, s, inner), BF16),
        scratch_shapes=[pltpu.VMEM((CHUNK + SUBLANES, inner), F32),
                        pltpu.VMEM((CHUNK + SUBLANES, gn), F32),
                        pltpu.VMEM((CHUNK + SUBLANES, gn), F32),
                        pltpu.VMEM((SSM_GROUPS, SSM_STATE, inner // SSM_GROUPS), F32),
                        pltpu.VMEM((CHUNK, inner), F32)],
        compiler_params=_cparams("parallel", "arbitrary"),
        name="ssd",
    )(proj3, proj3, proj3, proj3, proj3,
      cw[:, :inner], cw[:, inner:inner + gn], cw[:, inner + gn:],
      cbias[:, :inner], cbias[:, inner:inner + gn], cbias[:, inner + gn:],
      dtb, aneg, dexp, norm_g.astype(F32).reshape(1, inner), expand)


def _norm_t_kernel(x_ref, g_ref, o_ref):
    o_ref[...] = _pack_rows(_rms(x_ref[...], g_ref[...]).T)


def _norm_t(x, g, *, tm):
    t, d = x.shape
    return pl.pallas_call(
        _norm_t_kernel,
        grid=(t // tm,),
        in_specs=[pl.BlockSpec((tm, d), lambda i: (i, 0)), pl.BlockSpec((1, d), lambda i: (0, 0))],
        out_specs=pl.BlockSpec((d // 2, tm), lambda i: (0, i)),
        out_shape=jax.ShapeDtypeStruct((d // 2, t), jnp.uint32),
        compiler_params=_cparams("parallel"),
        name="peer_norm_t",
    )(x, g.reshape(1, d))


def _top_values(s, with_rank):
    vals = []
    cur = s
    rank = jnp.full(s.shape, float(PEER_TOPK), F32) if with_rank else None
    for r in range(PEER_TOPK):
        m = jnp.max(cur, axis=0, keepdims=True)
        vals.append(m)
        hit = cur == m
        if with_rank:
            rank = jnp.where(hit, float(r), rank)
        cur = jnp.where(hit, NEG_INF, cur)
    return jnp.concatenate(vals, axis=0), rank


def _dup_bf16(x):
    bits = pltpu.bitcast(x, jnp.uint32)
    hi = (bits + jnp.uint32(0x7FFF) + ((bits >> 16) & jnp.uint32(1))) & jnp.uint32(0xFFFF0000)
    return hi | (hi >> 16)


def _peer_select_kernel(hnt_ref, wqt_ref, keys_ref, c1_ref, w1_ref, r2_ref, w2_ref, qt_s):
    tl = hnt_ref.shape[1]
    kd = keys_ref.shape[2]
    qt_s[...] = jnp.dot(_unpack_rows(wqt_ref[...]), _unpack_rows(hnt_ref[...]),
                        preferred_element_type=F32)
    rk = lax.broadcasted_iota(jnp.int32, (PEER_TOPK, LANES), 0).astype(F32)

    def head(h, carry):
        for c in range(tl // LANES):
            ls = slice(c * LANES, (c + 1) * LANES)
            r1 = pl.multiple_of(h * (2 * kd), 2 * kd)
            q1 = qt_s[pl.ds(r1, kd), ls].astype(BF16)
            q2 = qt_s[pl.ds(r1 + kd, kd), ls].astype(BF16)
            s1 = jnp.dot(keys_ref[2 * h], q1, preferred_element_type=F32)
            s2 = jnp.dot(keys_ref[2 * h + 1], q2, preferred_element_type=F32)
            a, _ = _top_values(s1, False)
            b, rank2 = _top_values(s2, True)
            a0, b0 = a[0:1], b[0:1]
            best0 = a0 + b0
            cnt = jnp.zeros((PEER_TOPK, LANES), F32)
            cur = a + b0
            z = jnp.zeros((1, LANES), F32)
            for _ in range(PEER_TOPK):
                m = jnp.max(cur, axis=0, keepdims=True)
                idx = jnp.min(jnp.where(cur == m, rk, float(PEER_TOPK)), axis=0, keepdims=True)
                sel = rk == idx
                z = z + jnp.exp(m - best0)
                cnt = jnp.where(sel, cnt + 1.0, cnt)
                csel = jnp.max(jnp.where(sel, cnt, 0.0), axis=0, keepdims=True)
                nb = jnp.max(jnp.where(rk == csel, b, NEG_INF), axis=0, keepdims=True)
                cur = jnp.where(sel, a + nb, cur)
            cnt1 = jnp.zeros(s1.shape, F32)
            for r in range(PEER_TOPK):
                cnt1 = jnp.where(s1 == a[r:r + 1], cnt[r:r + 1], cnt1)
            c1_ref[h, :, ls] = _dup_bf16(cnt1)
            w1_ref[h, :, ls] = _dup_bf16(jnp.exp(s1 - a0))
            r2_ref[h, :, ls] = _pack_rows(rank2)
            w2_ref[h, :, ls] = _pack_rows(jnp.exp(s2 - b0) * (1.0 / z))
        return carry

    lax.fori_loop(0, PEER_HEADS, head, 0)


def _peer_select(hnt, wqt, keys, *, tl):
    dh, t = hnt.shape
    nqh = wqt.shape[0]
    nk = keys.shape[1]
    dup = jax.ShapeDtypeStruct((PEER_HEADS, nk, t), jnp.uint32)
    half = jax.ShapeDtypeStruct((PEER_HEADS, nk // 2, t), jnp.uint32)
    ospec = lambda: pl.BlockSpec((PEER_HEADS, nk, tl), lambda i: (0, 0, i))
    hspec = lambda: pl.BlockSpec((PEER_HEADS, nk // 2, tl), lambda i: (0, 0, i))
    return pl.pallas_call(
        _peer_select_kernel,
        grid=(t // tl,),
        in_specs=[pl.BlockSpec((dh, tl), lambda i: (0, i)),
                  pl.BlockSpec(wqt.shape, lambda i: (0, 0)),
                  pl.BlockSpec(keys.shape, lambda i: (0, 0, 0))],
        out_specs=[ospec(), ospec(), hspec(), hspec()],
        out_shape=[dup, dup, half, half],
        scratch_shapes=[pltpu.VMEM((2 * nqh, tl), F32)],
        compiler_params=_cparams("parallel"),
        name="peer_select",
    )(hnt, wqt, keys)


def _peer_dense_kernel(sub, hnt_ref, u_ref, vt_ref, c1_ref, w1_ref, r2_ref, w2_ref, x_ref, o_ref, acc_s, ht0, ht1, act0, act1):
    ht_b, act_b = (ht0, ht1), (act0, act1)
    tn = 2 * u_ref.shape[0]
    tm = hnt_ref.shape[1]
    j = pl.program_id(1)

    @pl.when(j == 0)
    def _():
        acc_s[...] = jnp.zeros_like(acc_s)

    per = sub // PEER_KEYS
    zero = jnp.zeros((PEER_KEYS, LANES), BF16)
    blocks = [(c2, k) for c2 in range(tm // sub) for k in range(tn // sub)]

    def row_bf16(ref, h, i1, ls):
        words = jnp.broadcast_to(ref[h, i1:i1 + 1, ls], (PEER_KEYS // 2, LANES))
        return pltpu.bitcast(words, BF16)

    def hidden(n):
        c2, k = blocks[n]
        u = _unpack_rows(u_ref[k * sub // 2:(k + 1) * sub // 2, :])
        ht_b[n % 2][...] = jnp.dot(u, _unpack_rows(hnt_ref[:, c2 * sub:(c2 + 1) * sub]), preferred_element_type=F32)

    def activate(n):
        c2, k = blocks[n]
        for cc in range(sub // LANES):
            ls = slice(c2 * sub + cc * LANES, c2 * sub + (cc + 1) * LANES)
            gs = [None] * per
            for h in range(PEER_HEADS):
                r2 = _unpack_rows(r2_ref[h, :, ls])
                w2 = _unpack_rows(w2_ref[h, :, ls])
                for il in range(per):
                    i1 = k * per + il
                    term = jnp.where(r2 < row_bf16(c1_ref, h, i1, ls), w2, zero) * row_bf16(w1_ref, h, i1, ls)
                    gs[il] = term if gs[il] is None else gs[il] + term
            for il in range(per):
                rs = slice(il * PEER_KEYS, (il + 1) * PEER_KEYS)
                cs = slice(cc * LANES, (cc + 1) * LANES)
                act_b[n % 2][rs, cs] = _gelu(ht_b[n % 2][rs, cs]).astype(BF16) * gs[il]

    def project(n):
        c2, k = blocks[n]
        vt = _unpack_rows(vt_ref[:, k * sub:(k + 1) * sub])
        acc_s[:, c2 * sub:(c2 + 1) * sub] += jnp.dot(vt, act_b[n % 2][...], preferred_element_type=F32)

    hidden(0)
    for n in range(len(blocks)):
        if n + 1 < len(blocks):
            hidden(n + 1)
        activate(n)
        if n >= 1:
            project(n - 1)
    project(len(blocks) - 1)

    @pl.when(j == pl.num_programs(1) - 1)
    def _():
        o_ref[...] = x_ref[...] + acc_s[...].T


def _peer_dense(x, hnt, u, vt, c1, w1, r2, w2, *, tm, tn, sub):
    t, d = x.shape
    e = 2 * u.shape[0]
    nk = c1.shape[1]
    rows = tn // PEER_KEYS
    return pl.pallas_call(
        functools.partial(_peer_dense_kernel, sub),
        grid=(t // tm, e // tn),
        in_specs=[pl.BlockSpec((d // 2, tm), lambda i, j: (0, i)),
                  pl.BlockSpec((tn // 2, d), lambda i, j: (j, 0)),
                  pl.BlockSpec((d // 2, tn), lambda i, j: (0, j)),
                  pl.BlockSpec((PEER_HEADS, rows, tm), lambda i, j: (0, j, i)),
                  pl.BlockSpec((PEER_HEADS, rows, tm), lambda i, j: (0, j, i)),
                  pl.BlockSpec((PEER_HEADS, nk // 2, tm), lambda i, j: (0, 0, i)),
                  pl.BlockSpec((PEER_HEADS, nk // 2, tm), lambda i, j: (0, 0, i)),
                  pl.BlockSpec((tm, d), lambda i, j: (i, 0))],
        out_specs=pl.BlockSpec((tm, d), lambda i, j: (i, 0)),
        out_shape=jax.ShapeDtypeStruct((t, d), F32),
        scratch_shapes=[pltpu.VMEM((d, tm), F32), pltpu.VMEM((sub, sub), F32), pltpu.VMEM((sub, sub), F32),
                        pltpu.VMEM((sub, sub), BF16), pltpu.VMEM((sub, sub), BF16)],
        compiler_params=_cparams("parallel", "arbitrary"),
        name="peer_dense",
    )(hnt, u, vt, c1, w1, r2, w2, x)


def _peer_layer(x, g, w_q, sub_keys, u_emb, v_emb, *, tm_norm, tl, tm, tn, sub):
    hd, two, nk, kd = sub_keys.shape
    hnt = _norm_t(x, g, tm=tm_norm)
    c1, w1, r2, w2 = _peer_select(hnt, _pack_rows_xla(w_q.T), sub_keys.reshape(hd * two, nk, kd).astype(BF16), tl=tl)
    return _peer_dense(x, hnt, _pack_rows_xla(u_emb), _pack_rows_xla(v_emb.T), c1, w1, r2, w2, tm=tm, tn=tn, sub=sub)


def _final_norm_kernel(x_ref, g_ref, o_ref):
    o_ref[...] = _rms(x_ref[...], g_ref[...])


def _final_norm(x, g, *, tm):
    t, d = x.shape
    return pl.pallas_call(
        _final_norm_kernel,
        grid=(t // tm,),
        in_specs=[pl.BlockSpec((tm, d), lambda i: (i, 0)), pl.BlockSpec((1, d), lambda i: (0, 0))],
        out_specs=pl.BlockSpec((tm, d), lambda i: (i, 0)),
        out_shape=jax.ShapeDtypeStruct((t, d), F32),
        compiler_params=_cparams("parallel"),
        name="final_norm",
    )(x, g.reshape(1, d))


def _rope_tables(s, dk):
    half = dk // 2
    inv = ROPE_BASE ** (-jnp.arange(half, dtype=F32) * (2.0 / dk))
    ang = jnp.arange(s, dtype=F32)[:, None] * inv[None, :]
    return jnp.cos(ang), jnp.sin(ang)


def _tile(n, want):
    if n <= want:
        return n
    for cand in range(want, 0, -LANES):
        if n % cand == 0:
            return cand
    return n


def _even_layer(x, b, s, g, w_in, conv_w, conv_b, w_a, b_a, w_i, b_i, lam, w_out):
    t, d = x.shape
    width = w_a.shape[0] * LRU_BLOCK
    n_in = w_in.shape[1]
    proj = _norm_matmul(x, g, w_in.astype(BF16), tm=_tile(t, 512), tn=_tile(n_in, 512))
    proj3 = proj.reshape(b, s, n_in)
    ct = 256
    ya = _lru_branch(proj3, conv_w.astype(F32), conv_b.astype(F32), _lru_gate_weights(w_a, w_i, ct),
                     b_a.astype(F32), b_i.astype(F32), lam.astype(F32), width=width, ct=ct)
    cos, sin = _rope_tables(s, RET_DIM)
    log_gamma = jnp.log1p(-jnp.exp2(-5.0 - jnp.arange(RET_HEADS, dtype=F32)))
    yb = _retention_branch(proj3, cos, sin, log_gamma, col0=2 * width)
    wo = w_out.astype(BF16)
    return _proj_residual(x, [ya.reshape(t, width), yb.reshape(t, -1)], [wo[:width], wo[width:]],
                          tm=_tile(t, 512), tn=_tile(d, 512))


def _odd_layer(x, b, s, g, w_in, conv_w, conv_b, dt_bias, a_log, d_skip, norm_g, w_out):
    t, d = x.shape
    inner = w_out.shape[0]
    n_in = w_in.shape[1]
    tn = 512
    n_pad = -(-n_in // tn) * tn
    w_in_p = jnp.pad(w_in.astype(BF16), ((0, 0), (0, n_pad - n_in)))
    proj = _norm_matmul(x, g, w_in_p, tm=_tile(t, 512), tn=tn)
    y = _ssd_core(proj.reshape(b, s, n_pad), conv_w, conv_b, dt_bias, a_log, d_skip, norm_g, inner=inner)
    return _proj_residual(x, [y.reshape(t, inner)], [w_out.astype(BF16)], tm=_tile(t, 512), tn=_tile(d, 512))


def kernel(x, mix_norm, ffn_norm, final_norm, even_w_in, lru_conv_w, lru_conv_b, lru_w_a, lru_b_a, lru_w_i,
           lru_b_i, lru_lambda, even_w_out, ssm_w_in, ssm_conv_w, ssm_conv_b, ssm_dt_bias, ssm_a_log, ssm_d,
           ssm_norm, ssm_w_out, peer_w_q, peer_sub_keys, peer_u, peer_v):
    b, s, d = x.shape
    t = b * s
    depth = mix_norm.shape[0]
    h = x.reshape(t, d).astype(F32)
    for layer in range(depth):
        j = layer // 2
        if layer % 2 == 0:
            h = _even_layer(h, b, s, mix_norm[layer], even_w_in[j], lru_conv_w[j], lru_conv_b[j], lru_w_a[j],
                            lru_b_a[j], lru_w_i[j], lru_b_i[j], lru_lambda[j], even_w_out[j])
        else:
            h = _odd_layer(h, b, s, mix_norm[layer], ssm_w_in[j], ssm_conv_w[j], ssm_conv_b[j], ssm_dt_bias[j],
                           ssm_a_log[j], ssm_d[j], ssm_norm[j], ssm_w_out[j])
        h = _peer_layer(h, ffn_norm[layer], peer_w_q[layer], peer_sub_keys[layer], peer_u[layer], peer_v[layer],
                        tm_norm=_tile(t, 512), tl=_tile(t, 512), tm=_tile(t, 1024), tn=1024, sub=256)
    return _final_norm(h, final_norm, tm=_tile(t, 512)).reshape(b, s, d).astype(x.dtype)
```

```python
import functools
import math

import jax
import jax.numpy as jnp
from jax import lax
from jax.experimental import pallas as pl
from jax.experimental.pallas import tpu as pltpu

F32 = jnp.float32
BF16 = jnp.bfloat16

EPS = 1e-6
LANES = 128
SUBLANES = 8
VMEM_LIMIT = 56 * 1024 * 1024

LRU_BLOCK = 64
LRU_C = 8.0
CONV_WIDTH = 4
RET_HEADS = 4
RET_DIM = 256
CHUNK = 128
ROPE_BASE = 10000.0
SSM_HEAD_DIM = 64
SSM_GROUPS = 4
SSM_STATE = 128
PEER_HEADS = 8
PEER_KEYS = 128
PEER_TOPK = 16
NEG_INF = float("-inf")
POS_INF = float("inf")


def _cparams(*sem):
    return pltpu.CompilerParams(dimension_semantics=sem, vmem_limit_bytes=VMEM_LIMIT)


def _softplus(x):
    return jnp.maximum(x, 0.0) + jnp.log1p(jnp.exp(-jnp.abs(x)))


def _sigmoid(x):
    return 1.0 / (1.0 + jnp.exp(-x))


def _silu(x):
    return x * _sigmoid(x)


def _gelu(x):
    c = math.sqrt(2.0 / math.pi)
    hx = 0.5 * x
    return hx + hx * jnp.tanh(x * (c + (c * 0.044715) * (x * x)))


def _pack_rows(x):
    return pltpu.bitcast(x.astype(BF16), jnp.uint32)


def _unpack_rows(x):
    return pltpu.bitcast(x, BF16)


def _rms(x, g):
    return x * lax.rsqrt(jnp.mean(x * x, axis=-1, keepdims=True) + EPS) * g


def _pack_weight_kernel(transpose, w_ref, o_ref):
    w = w_ref[...]
    o_ref[...] = _pack_rows(w.T if transpose else w)


def _pack_weight(w, *, transpose=False, rows=512):
    r, c = w.shape
    rt = _tile(r, rows)
    if transpose:
        out_spec, out_shape = pl.BlockSpec((c // 2, rt), lambda i: (0, i)), (c // 2, r)
    else:
        out_spec, out_shape = pl.BlockSpec((rt // 2, c), lambda i: (i, 0)), (r // 2, c)
    return pl.pallas_call(
        functools.partial(_pack_weight_kernel, transpose),
        grid=(r // rt,),
        in_specs=[pl.BlockSpec((rt, c), lambda i: (i, 0))],
        out_specs=out_spec,
        out_shape=jax.ShapeDtypeStruct(out_shape, jnp.uint32),
        compiler_params=_cparams("parallel"),
        name="pack_weight",
    )(w.astype(F32))


def _norm_matmul_kernel(x_ref, g_ref, w_ref, o_ref, hn_ref):
    @pl.when(pl.program_id(1) == 0)
    def _():
        hn_ref[...] = _rms(x_ref[...], g_ref[...]).astype(BF16)

    o_ref[...] = _pack_rows(jnp.dot(hn_ref[...], _unpack_rows(w_ref[...]), preferred_element_type=F32))


def _norm_matmul(x, g, w, *, tm, tn):
    t, d = x.shape
    n = w.shape[1]
    return pl.pallas_call(
        _norm_matmul_kernel,
        grid=(t // tm, n // tn),
        in_specs=[pl.BlockSpec((tm, d), lambda i, j: (i, 0)),
                  pl.BlockSpec((1, d), lambda i, j: (0, 0)),
                  pl.BlockSpec((d // 2, tn), lambda i, j: (0, j))],
        out_specs=pl.BlockSpec((tm // 2, tn), lambda i, j: (i, j)),
        out_shape=jax.ShapeDtypeStruct((t // 2, n), jnp.uint32),
        scratch_shapes=[pltpu.VMEM((tm, d), BF16)],
        compiler_params=_cparams("parallel", "arbitrary"),
        name="norm_matmul",
    )(x, g.reshape(1, d), w)


def _proj_residual_kernel(n_in, x_ref, *refs):
    o_ref = refs[2 * n_in]
    acc = x_ref[...]
    for k in range(n_in):
        acc = acc + jnp.dot(_unpack_rows(refs[k][...]), _unpack_rows(refs[n_in + k][...]),
                            preferred_element_type=F32)
    o_ref[...] = acc


def _proj_residual(x, ys, ws, *, tm, tn):
    t, d = x.shape
    n_in = len(ys)
    in_specs = [pl.BlockSpec((tm, tn), lambda i, j: (i, j))]
    in_specs += [pl.BlockSpec((tm // 2, y.shape[1]), lambda i, j: (i, 0)) for y in ys]
    in_specs += [pl.BlockSpec((y.shape[1] // 2, tn), functools.partial(lambda rb, i, j: (rb, j), rb))
                 for y, (_, rb) in zip(ys, ws)]
    return pl.pallas_call(
        functools.partial(_proj_residual_kernel, n_in),
        grid=(t // tm, d // tn),
        in_specs=in_specs,
        out_specs=pl.BlockSpec((tm, tn), lambda i, j: (i, j)),
        out_shape=jax.ShapeDtypeStruct((t, d), F32),
        compiler_params=_cparams("parallel", "parallel"),
        name="proj_residual",
    )(x, *ys, *[w for w, _ in ws])


def _lru_kernel(ga_ref, xa_ref, cw_ref, cb_ref, wg_ref, ba_ref, bi_ref, lam_ref, o_ref, a_s, u_s):
    s, ct = 2 * xa_ref.shape[1], xa_ref.shape[2]
    x = _unpack_rows(xa_ref[0]).astype(F32)
    row = lax.broadcasted_iota(jnp.int32, (s, ct), 0)
    cw = cw_ref[...]
    xc = x * cw[CONV_WIDTH - 1:CONV_WIDTH, :] + cb_ref[...]
    for k in range(1, CONV_WIDTH):
        xs = jnp.where(row >= k, pltpu.roll(x, k, axis=0), 0.0)
        xc = xc + xs * cw[CONV_WIDTH - 1 - k:CONV_WIDTH - k, :]
    gates = jnp.dot(xc.astype(BF16), wg_ref[0], preferred_element_type=F32)
    r = _sigmoid(gates[:, :ct] + ba_ref[...])
    i = _sigmoid(gates[:, ct:] + bi_ref[...])
    log_a = (-LRU_C * r) * _softplus(-lam_ref[...])
    a_s[...] = jnp.exp(log_a)
    th = jnp.tanh(log_a)
    u_s[...] = jnp.sqrt(-2.0 * th / (1.0 - th)) * (i * xc)

    sub = lax.broadcasted_iota(jnp.int32, (SUBLANES, ct), 0)

    def local_scan(r0):
        a = a_s[pl.ds(r0, SUBLANES), :]
        u = u_s[pl.ds(r0, SUBLANES), :]
        for sh in (1, 2, 4):
            keep = sub >= sh
            u = jnp.where(keep, a * pltpu.roll(u, sh, axis=0) + u, u)
            a = jnp.where(keep, a * pltpu.roll(a, sh, axis=0), a)
        return a, u

    def tile_pair(t, h):
        r0 = pl.multiple_of(t * (2 * SUBLANES), 2 * SUBLANES)
        a1, u1 = local_scan(r0)
        a2, u2 = local_scan(r0 + SUBLANES)
        h1 = u1 + a1 * h
        h2 = u2 + a2 * h1[SUBLANES - 1:SUBLANES, :]
        p0 = pl.multiple_of(t * SUBLANES, SUBLANES)
        gate = _gelu(_unpack_rows(ga_ref[0, pl.ds(p0, SUBLANES), :]).astype(F32))
        o_ref[0, pl.ds(p0, SUBLANES), :] = _pack_rows(gate * jnp.concatenate([h1, h2], axis=0))
        return h2[SUBLANES - 1:SUBLANES, :]

    lax.fori_loop(0, s // (2 * SUBLANES), tile_pair, jnp.zeros((1, ct), F32), unroll=2)


def _lru_branch(proj3, conv_w, conv_b, w_gates, b_a, b_i, lam, *, width, ct):
    b, sh, _ = proj3.shape
    s = 2 * sh
    nct = width // ct
    vec = lambda: pl.BlockSpec((1, ct), lambda bi, j: (0, j))
    return pl.pallas_call(
        _lru_kernel,
        grid=(b, nct),
        in_specs=[pl.BlockSpec((1, sh, ct), lambda bi, j: (bi, 0, j)),
                  pl.BlockSpec((1, sh, ct), lambda bi, j: (bi, 0, nct + j)),
                  pl.BlockSpec((CONV_WIDTH, ct), lambda bi, j: (0, j)),
                  vec(),
                  pl.BlockSpec((1, ct, 2 * ct), lambda bi, j: (j, 0, 0)),
                  vec(), vec(), vec()],
        out_specs=pl.BlockSpec((1, sh, ct), lambda bi, j: (bi, 0, j)),
        out_shape=jax.ShapeDtypeStruct((b, sh, width), jnp.uint32),
        scratch_shapes=[pltpu.VMEM((s, ct), F32), pltpu.VMEM((s, ct), F32)],
        compiler_params=_cparams("parallel", "parallel"),
        name="rg_lru",
    )(proj3, proj3, conv_w, conv_b.reshape(1, -1), w_gates, b_a.reshape(1, -1), b_i.reshape(1, -1),
      lam.reshape(1, -1))


def _lru_gate_weights(w_a, w_i, ct):
    nb = w_a.shape[0]
    per = ct // LRU_BLOCK

    def bd(w):
        w = w.reshape(nb // per, per, LRU_BLOCK, LRU_BLOCK)
        eye = jnp.eye(per, dtype=w.dtype)
        return jnp.einsum('tpij,pq->tpiqj', w, eye).reshape(nb // per, ct, ct)

    return jnp.concatenate([bd(w_a), bd(w_i)], axis=-1).astype(BF16)


def _retention_kernel(lg_ref, q_ref, k_ref, v_ref, gb_ref, cos_ref, sin_ref, o_ref, q_s, k_s, kd_s, st_s):
    s, dk = 2 * q_ref.shape[1], q_ref.shape[2]
    half = dk // 2
    lg = lg_ref[pl.program_id(1)]
    cos = cos_ref[...]
    sin = sin_ref[...]

    def rope(x):
        x1, x2 = x[:, :half], x[:, half:]
        return jnp.concatenate([x1 * cos - x2 * sin, x1 * sin + x2 * cos], axis=-1)

    q_s[...] = (rope(_unpack_rows(q_ref[0]).astype(F32)) * (dk ** -0.5)).astype(BF16)
    k_s[...] = rope(_unpack_rows(k_ref[0]).astype(F32))
    st_s[...] = jnp.zeros_like(st_s)

    ri = lax.broadcasted_iota(jnp.int32, (CHUNK, CHUNK), 0)
    ci = lax.broadcasted_iota(jnp.int32, (CHUNK, CHUNK), 1)
    rel = (ri - ci).astype(F32)
    decay_in = jnp.where(rel >= 0, jnp.exp(lg * jnp.maximum(rel, 0.0)), 0.0)
    rowf = lax.broadcasted_iota(jnp.int32, (CHUNK, dk), 0).astype(F32)
    k_decay = jnp.exp(lg * (CHUNK - 1.0 - rowf))
    q_decay = jnp.exp(lg * (rowf + 1.0))
    chunk_decay = jnp.exp(jnp.full((1, dk), lg * CHUNK, F32))

    def chunk(c, carry):
        r0 = pl.multiple_of(c * CHUNK, CHUNK)
        qc = q_s[pl.ds(r0, CHUNK), :]
        kc = k_s[pl.ds(r0, CHUNK), :]
        p0 = pl.multiple_of(c * (CHUNK // 2), CHUNK // 2)
        vc = _unpack_rows(v_ref[0, pl.ds(p0, CHUNK // 2), :])
        scores = lax.dot_general(qc, kc.astype(BF16), (((1,), (1,)), ((), ())), preferred_element_type=F32)
        y = jnp.dot((scores * decay_in).astype(BF16), vc, preferred_element_type=F32)
        state = st_s[...]
        y = y + jnp.dot(qc, state.astype(BF16), preferred_element_type=F32) * q_decay
        kd_s[...] = (kc * k_decay).T.astype(BF16)
        st_s[...] = state * chunk_decay + jnp.dot(kd_s[...], vc, preferred_element_type=F32)
        y = y * lax.rsqrt(jnp.mean(y * y, axis=-1, keepdims=True) + EPS)
        gate = _silu(_unpack_rows(gb_ref[0, pl.ds(p0, CHUNK // 2), :]).astype(F32))
        o_ref[0, pl.ds(p0, CHUNK // 2), :] = _pack_rows(gate * y)
        return carry

    lax.fori_loop(0, s // CHUNK, chunk, 0)


def _retention_branch(proj3, cos, sin, log_gamma, *, col0):
    b, sh, _ = proj3.shape
    s = 2 * sh
    hd = RET_HEADS
    base = col0 // RET_DIM
    blk = lambda off: pl.BlockSpec((1, sh, RET_DIM), lambda bi, h: (bi, 0, base + off + h))
    tab = lambda: pl.BlockSpec((s, RET_DIM // 2), lambda bi, h: (0, 0))
    return pl.pallas_call(
        _retention_kernel,
        grid=(b, hd),
        in_specs=[pl.BlockSpec(memory_space=pltpu.SMEM),
                  blk(0), blk(hd), blk(2 * hd), blk(3 * hd), tab(), tab()],
        out_specs=pl.BlockSpec((1, sh, RET_DIM), lambda bi, h: (bi, 0, h)),
        out_shape=jax.ShapeDtypeStruct((b, sh, hd * RET_DIM), jnp.uint32),
        scratch_shapes=[pltpu.VMEM((s, RET_DIM), BF16), pltpu.VMEM((s, RET_DIM), F32),
                        pltpu.VMEM((RET_DIM, CHUNK), BF16), pltpu.VMEM((RET_DIM, RET_DIM), F32)],
        compiler_params=_cparams("parallel", "parallel"),
        name="retention",
    )(log_gamma, proj3, proj3, proj3, proj3, cos, sin)


def _ssd_kernel(z_ref, xs_ref, bm_ref, cm_ref, dt_ref, cwx_ref, cwb_ref, cwc_ref, cbx_ref, cbb_ref, cbc_ref,
                dtb_ref, aneg_ref, dexp_ref, ng_ref, exp_ref, o_ref, ex_s, eb_s, ec_s, st_s, y_s):
    inner = xs_ref.shape[2]
    gw = inner // SSM_GROUPS
    hpg = gw // SSM_HEAD_DIM

    @pl.when(pl.program_id(1) == 0)
    def _():
        ex_s[0:SUBLANES, :] = jnp.zeros((SUBLANES, ex_s.shape[1]), F32)
        eb_s[0:SUBLANES, :] = jnp.zeros((SUBLANES, eb_s.shape[1]), F32)
        ec_s[0:SUBLANES, :] = jnp.zeros((SUBLANES, ec_s.shape[1]), F32)
        st_s[...] = jnp.zeros_like(st_s)

    def conv_silu(src_ref, ext, cw_ref, cb_ref):
        ext[SUBLANES:SUBLANES + CHUNK, :] = _unpack_rows(src_ref[0]).astype(F32)
        cw = cw_ref[...]
        acc = cb_ref[...] + ext[SUBLANES:SUBLANES + CHUNK, :] * cw[CONV_WIDTH - 1:CONV_WIDTH, :]
        for k in range(1, CONV_WIDTH):
            acc = acc + ext[SUBLANES - k:SUBLANES - k + CHUNK, :] * cw[CONV_WIDTH - 1 - k:CONV_WIDTH - k, :]
        ext[0:SUBLANES, :] = ext[CHUNK:CHUNK + SUBLANES, :]
        return _silu(acc)

    xs = conv_silu(xs_ref, ex_s, cwx_ref, cbx_ref)
    bm = conv_silu(bm_ref, eb_s, cwb_ref, cbb_ref)
    cm = conv_silu(cm_ref, ec_s, cwc_ref, cbc_ref)

    dt = _softplus(_unpack_rows(dt_ref[0]).astype(F32) + dtb_ref[...])
    a = dt * aneg_ref[...]
    row = lax.broadcasted_iota(jnp.int32, (CHUNK, LANES), 0)
    a_cum = a
    sh = 1
    while sh < CHUNK:
        a_cum = a_cum + jnp.where(row >= sh, pltpu.roll(a_cum, sh, axis=0), 0.0)
        sh *= 2
    a_cum_t = a_cum.T

    expand = exp_ref[...]
    dt_e = jnp.dot(dt, expand, preferred_element_type=F32, precision=lax.Precision.HIGHEST)
    ac_e = jnp.dot(a_cum, expand, preferred_element_type=F32, precision=lax.Precision.HIGHEST)
    ac_last = ac_e[CHUNK - 1:CHUNK, :]
    xdt = xs * dt_e
    xds = (xdt * jnp.exp(ac_last - ac_e)).astype(BF16)
    ea_e = jnp.exp(ac_e)
    cdec = jnp.exp(ac_last)
    xdt_b = xdt.astype(BF16)

    li = lax.broadcasted_iota(jnp.int32, (CHUNK, CHUNK), 0)
    si = lax.broadcasted_iota(jnp.int32, (CHUNK, CHUNK), 1)
    causal = li >= si
    lane = lax.broadcasted_iota(jnp.int32, (CHUNK, LANES), 1)
    first_half = lane < SSM_HEAD_DIM

    for g in range(SSM_GROUPS):
        bg = bm[:, g * SSM_STATE:(g + 1) * SSM_STATE]
        cg = cm[:, g * SSM_STATE:(g + 1) * SSM_STATE].astype(BF16)
        bg_b = bg.astype(BF16)
        cb = lax.dot_general(cg, bg_b, (((1,), (1,)), ((), ())), preferred_element_type=F32)
        state = st_s[g]
        y_off = jnp.dot(cg, state.astype(BF16), preferred_element_type=F32) * ea_e[:, g * gw:(g + 1) * gw]
        new_states = jnp.dot(bg.T.astype(BF16), xds[:, g * gw:(g + 1) * gw], preferred_element_type=F32)
        st_s[g] = state * cdec[:, g * gw:(g + 1) * gw] + new_states
        for pr in range(hpg // 2):
            c0 = g * gw + pr * LANES
            xp = xdt_b[:, c0:c0 + LANES]
            ys = []
            for hh in (2 * pr, 2 * pr + 1):
                hd = g * hpg + hh
                seg = a_cum[:, hd:hd + 1] - a_cum_t[hd:hd + 1, :]
                lmat = jnp.exp(jnp.where(causal, seg, -1e30))
                ys.append(jnp.dot((cb * lmat).astype(BF16), xp, preferred_element_type=F32))
            y_s[:, c0:c0 + LANES] = jnp.where(first_half, ys[0], ys[1]) + y_off[:, pr * LANES:(pr + 1) * LANES]

    y = y_s[...] + dexp_ref[...] * xs
    y = y * _silu(_unpack_rows(z_ref[0]).astype(F32))
    o_ref[0] = _pack_rows(_rms(y, ng_ref[...]))


def _ssd_core(proj3, conv_w, conv_b, dt_bias, a_log, d_skip, norm_g, *, inner):
    b, sh, _ = proj3.shape
    ch = CHUNK // 2
    gn = SSM_GROUPS * SSM_STATE
    heads = inner // SSM_HEAD_DIM
    pad = LANES - heads
    dtb = jnp.pad(dt_bias.astype(F32), (0, pad)).reshape(1, LANES)
    aneg = jnp.pad(-jnp.exp(a_log.astype(F32)), (0, pad)).reshape(1, LANES)
    dexp = jnp.repeat(d_skip.astype(F32), SSM_HEAD_DIM).reshape(1, inner)
    expand = (jnp.arange(LANES)[:, None] == (jnp.arange(inner) // SSM_HEAD_DIM)[None, :]).astype(F32)
    cw = conv_w.astype(F32)
    cbias = conv_b.astype(F32).reshape(1, -1)
    full = lambda shape: pl.BlockSpec(shape, lambda bi, c: tuple(0 for _ in shape))
    return pl.pallas_call(
        _ssd_kernel,
        grid=(b, sh // ch),
        in_specs=[pl.BlockSpec((1, ch, inner), lambda bi, c: (bi, c, 0)),
                  pl.BlockSpec((1, ch, inner), lambda bi, c: (bi, c, 1)),
                  pl.BlockSpec((1, ch, gn), lambda bi, c: (bi, c, 2 * inner // gn)),
                  pl.BlockSpec((1, ch, gn), lambda bi, c: (bi, c, 2 * inner // gn + 1)),
                  pl.BlockSpec((1, ch, LANES), lambda bi, c: (bi, c, (2 * inner + 2 * gn) // LANES)),
                  full((CONV_WIDTH, inner)), full((CONV_WIDTH, gn)), full((CONV_WIDTH, gn)),
                  full((1, inner)), full((1, gn)), full((1, gn)),
                  full((1, LANES)), full((1, LANES)), full((1, inner)), full((1, inner)),
                  full((LANES, inner))],
        out_specs=pl.BlockSpec((1, ch, inner), lambda bi, c: (bi, c, 0)),
        out_shape=jax.ShapeDtypeStruct((b, sh, inner), jnp.uint32),
        scratch_shapes=[pltpu.VMEM((CHUNK + SUBLANES, inner), F32),
                        pltpu.VMEM((CHUNK + SUBLANES, gn), F32),
                        pltpu.VMEM((CHUNK + SUBLANES, gn), F32),
                        pltpu.VMEM((SSM_GROUPS, SSM_STATE, inner // SSM_GROUPS), F32),
                        pltpu.VMEM((CHUNK, inner), F32)],
        compiler_params=_cparams("parallel", "arbitrary"),
        name="ssd",
    )(proj3, proj3, proj3, proj3, proj3,
      cw[:, :inner], cw[:, inner:inner + gn], cw[:, inner + gn:],
      cbias[:, :inner], cbias[:, inner:inner + gn], cbias[:, inner + gn:],
      dtb, aneg, dexp, norm_g.astype(F32).reshape(1, inner), expand)


def _norm_t_kernel(x_ref, g_ref, o_ref):
    o_ref[...] = _pack_rows(_rms(x_ref[...], g_ref[...]).T)


def _norm_t(x, g, *, tm):
    t, d = x.shape
    return pl.pallas_call(
        _norm_t_kernel,
        grid=(t // tm,),
        in_specs=[pl.BlockSpec((tm, d), lambda i: (i, 0)), pl.BlockSpec((1, d), lambda i: (0, 0))],
        out_specs=pl.BlockSpec((d // 2, tm), lambda i: (0, i)),
        out_shape=jax.ShapeDtypeStruct((d // 2, t), jnp.uint32),
        compiler_params=_cparams("parallel"),
        name="peer_norm_t",
    )(x, g.reshape(1, d))


def _top_values(s, with_rank):
    vals = []
    cur = s
    rank = jnp.full(s.shape, float(PEER_TOPK), F32) if with_rank else None
    for r in range(PEER_TOPK):
        m = jnp.max(cur, axis=0, keepdims=True)
        vals.append(m)
        hit = cur == m
        if with_rank:
            rank = jnp.where(hit, float(r), rank)
        cur = jnp.where(hit, NEG_INF, cur)
    return jnp.concatenate(vals, axis=0), rank


def _dup_bf16(x):
    bits = pltpu.bitcast(x, jnp.uint32)
    hi = (bits + jnp.uint32(0x7FFF) + ((bits >> 16) & jnp.uint32(1))) & jnp.uint32(0xFFFF0000)
    return hi | (hi >> 16)


def _peer_select_kernel(hnt_ref, wqt_ref, keys_ref, c1_ref, w1_ref, r2_ref, w2_ref, qt_s):
    tl = hnt_ref.shape[1]
    kd = keys_ref.shape[2]
    qt_s[...] = jnp.dot(_unpack_rows(wqt_ref[...]), _unpack_rows(hnt_ref[...]),
                        preferred_element_type=F32)
    rk = lax.broadcasted_iota(jnp.int32, (PEER_TOPK, LANES), 0).astype(F32)

    def head(h, carry):
        for c in range(tl // LANES):
            ls = slice(c * LANES, (c + 1) * LANES)
            r1 = pl.multiple_of(h * (2 * kd), 2 * kd)
            q1 = qt_s[pl.ds(r1, kd), ls].astype(BF16)
            q2 = qt_s[pl.ds(r1 + kd, kd), ls].astype(BF16)
            s1 = jnp.dot(keys_ref[2 * h], q1, preferred_element_type=F32)
            s2 = jnp.dot(keys_ref[2 * h + 1], q2, preferred_element_type=F32)
            a, _ = _top_values(s1, False)
            b, rank2 = _top_values(s2, True)
            a0, b0 = a[0:1], b[0:1]
            best0 = a0 + b0
            cnt = jnp.zeros((PEER_TOPK, LANES), F32)
            cur = a + b0
            z = jnp.zeros((1, LANES), F32)
            for _ in range(PEER_TOPK):
                m = jnp.max(cur, axis=0, keepdims=True)
                idx = jnp.min(jnp.where(cur == m, rk, float(PEER_TOPK)), axis=0, keepdims=True)
                sel = rk == idx
                z = z + jnp.exp(m - best0)
                cnt = jnp.where(sel, cnt + 1.0, cnt)
                csel = jnp.max(jnp.where(sel, cnt, 0.0), axis=0, keepdims=True)
                nb = jnp.max(jnp.where(rk == csel, b, NEG_INF), axis=0, keepdims=True)
                cur = jnp.where(sel, a + nb, cur)
            cnt1 = jnp.zeros(s1.shape, F32)
            for r in range(PEER_TOPK):
                cnt1 = jnp.where(s1 == a[r:r + 1], cnt[r:r + 1], cnt1)
            c1_ref[h, :, ls] = _dup_bf16(cnt1)
            w1_ref[h, :, ls] = _dup_bf16(jnp.exp(s1 - a0))
            r2_ref[h, :, ls] = _pack_rows(rank2)
            w2_ref[h, :, ls] = _pack_rows(jnp.exp(s2 - b0) * (1.0 / z))
        return carry

    lax.fori_loop(0, PEER_HEADS, head, 0)


def _peer_select(hnt, wqt, keys, *, tl):
    dh, t = hnt.shape
    nqh = wqt.shape[0]
    nk = keys.shape[1]
    dup = jax.ShapeDtypeStruct((PEER_HEADS, nk, t), jnp.uint32)
    half = jax.ShapeDtypeStruct((PEER_HEADS, nk // 2, t), jnp.uint32)
    ospec = lambda: pl.BlockSpec((PEER_HEADS, nk, tl), lambda i: (0, 0, i))
    hspec = lambda: pl.BlockSpec((PEER_HEADS, nk // 2, tl), lambda i: (0, 0, i))
    return pl.pallas_call(
        _peer_select_kernel,
        grid=(t // tl,),
        in_specs=[pl.BlockSpec((dh, tl), lambda i: (0, i)),
                  pl.BlockSpec(wqt.shape, lambda i: (0, 0)),
                  pl.BlockSpec(keys.shape, lambda i: (0, 0, 0))],
        out_specs=[ospec(), ospec(), hspec(), hspec()],
        out_shape=[dup, dup, half, half],
        scratch_shapes=[pltpu.VMEM((2 * nqh, tl), F32)],
        compiler_params=_cparams("parallel"),
        name="peer_select",
    )(hnt, wqt, keys)


def _peer_dense_kernel(sub, hnt_ref, u_ref, vt_ref, c1_ref, w1_ref, r2_ref, w2_ref, x_ref, zero_ref, o_ref, acc_s, ht0, ht1, act0, act1, g0, g1):
    ht_b, act_b, g_b = (ht0, ht1), (act0, act1), (g0, g1)
    tn = 2 * u_ref.shape[0]
    tm = hnt_ref.shape[1]
    j = pl.program_id(1)

    @pl.when(j == 0)
    def _():
        acc_s[...] = jnp.zeros_like(acc_s)

    per = sub // PEER_KEYS
    zero = jnp.zeros((PEER_KEYS, LANES), BF16)
    blocks = [(c2, k) for c2 in range(tm // sub) for k in range(tn // sub)]

    def row_bf16(ref, h, i1, ls):
        words = jnp.broadcast_to(ref[h, i1:i1 + 1, ls], (PEER_KEYS // 2, LANES))
        return pltpu.bitcast(words, BF16)

    deps = {}

    def hidden(n):
        c2, k = blocks[n]
        u = _unpack_rows(u_ref[k * sub // 2:(k + 1) * sub // 2, :])
        hw = hnt_ref[:, c2 * sub:(c2 + 1) * sub]
        if n - 1 in deps:
            gate0 = hw[0:LANES, :] | jnp.tile(deps.pop(n - 1), (LANES // SUBLANES, sub // LANES))
            hw = jnp.concatenate([gate0, hw[LANES:, :]], axis=0)
        ht_b[n % 2][...] = jnp.dot(u, _unpack_rows(hw), preferred_element_type=F32)

    def gates(n):
        c2, k = blocks[n]
        live = None
        for cc in range(sub // LANES):
            ls = slice(c2 * sub + cc * LANES, c2 * sub + (cc + 1) * LANES)
            gs = [None] * per
            for h in range(PEER_HEADS):
                r2 = _unpack_rows(r2_ref[h, :, ls])
                w2 = _unpack_rows(w2_ref[h, :, ls])
                for il in range(per):
                    i1 = k * per + il
                    term = jnp.where(r2 < row_bf16(c1_ref, h, i1, ls), w2, zero) * row_bf16(w1_ref, h, i1, ls)
                    gs[il] = term if gs[il] is None else gs[il] + term
            for il in range(per):
                g_b[n % 2][il * PEER_KEYS:(il + 1) * PEER_KEYS, cc * LANES:(cc + 1) * LANES] = gs[il]
                bits = pltpu.bitcast(gs[il], jnp.uint32)
                for r in range(bits.shape[0] // SUBLANES):
                    piece = bits[r * SUBLANES:(r + 1) * SUBLANES, :]
                    live = piece if live is None else live | piece
        deps[n] = live & zero_ref[...]

    def activate(n):
        act_b[n % 2][...] = _gelu(ht_b[n % 2][...]).astype(BF16) * g_b[n % 2][...]

    def project(n):
        c2, k = blocks[n]
        vt = _unpack_rows(vt_ref[:, k * sub:(k + 1) * sub])
        acc_s[:, c2 * sub:(c2 + 1) * sub] += jnp.dot(vt, act_b[n % 2][...], preferred_element_type=F32)

    hidden(0)
    gates(0)
    for n in range(len(blocks)):
        if n + 1 < len(blocks):
            hidden(n + 1)
            gates(n + 1)
        activate(n)
        if n >= 1:
            project(n - 1)
    project(len(blocks) - 1)

    @pl.when(j == pl.num_programs(1) - 1)
    def _():
        o_ref[...] = x_ref[...] + acc_s[...].T


def _peer_dense(x, hnt, u, vt, c1, w1, r2, w2, *, tm, tn, sub):
    t, d = x.shape
    e = 2 * u.shape[0]
    nk = c1.shape[1]
    rows = tn // PEER_KEYS
    return pl.pallas_call(
        functools.partial(_peer_dense_kernel, sub),
        grid=(t // tm, e // tn),
        in_specs=[pl.BlockSpec((d // 2, tm), lambda i, j: (0, i)),
                  pl.BlockSpec((tn // 2, d), lambda i, j: (j, 0)),
                  pl.BlockSpec((d // 2, tn), lambda i, j: (0, j)),
                  pl.BlockSpec((PEER_HEADS, rows, tm), lambda i, j: (0, j, i)),
                  pl.BlockSpec((PEER_HEADS, rows, tm), lambda i, j: (0, j, i)),
                  pl.BlockSpec((PEER_HEADS, nk // 2, tm), lambda i, j: (0, 0, i)),
                  pl.BlockSpec((PEER_HEADS, nk // 2, tm), lambda i, j: (0, 0, i)),
                  pl.BlockSpec((tm, d), lambda i, j: (i, 0)),
                  pl.BlockSpec((SUBLANES, LANES), lambda i, j: (0, 0))],
        out_specs=pl.BlockSpec((tm, d), lambda i, j: (i, 0)),
        out_shape=jax.ShapeDtypeStruct((t, d), F32),
        scratch_shapes=[pltpu.VMEM((d, tm), F32), pltpu.VMEM((sub, sub), F32), pltpu.VMEM((sub, sub), F32),
                        pltpu.VMEM((sub, sub), BF16), pltpu.VMEM((sub, sub), BF16),
                        pltpu.VMEM((sub, sub), BF16), pltpu.VMEM((sub, sub), BF16)],
        compiler_params=_cparams("parallel", "arbitrary"),
        name="peer_dense",
    )(hnt, u, vt, c1, w1, r2, w2, x, jnp.zeros((SUBLANES, LANES), jnp.uint32))


def _peer_layer(x, g, w_q, sub_keys, u_emb, v_emb, *, tm_norm, tl, tm, tn, sub):
    hd, two, nk, kd = sub_keys.shape
    hnt = _norm_t(x, g, tm=tm_norm)
    c1, w1, r2, w2 = _peer_select(hnt, _pack_weight(w_q, transpose=True),
                                  sub_keys.reshape(hd * two, nk, kd).astype(BF16), tl=tl)
    return _peer_dense(x, hnt, _pack_weight(u_emb), _pack_weight(v_emb, transpose=True), c1, w1, r2, w2,
                       tm=tm, tn=tn, sub=sub)


def _final_norm_kernel(x_ref, g_ref, o_ref):
    o_ref[...] = _rms(x_ref[...], g_ref[...])


def _final_norm(x, g, *, tm):
    t, d = x.shape
    return pl.pallas_call(
        _final_norm_kernel,
        grid=(t // tm,),
        in_specs=[pl.BlockSpec((tm, d), lambda i: (i, 0)), pl.BlockSpec((1, d), lambda i: (0, 0))],
        out_specs=pl.BlockSpec((tm, d), lambda i: (i, 0)),
        out_shape=jax.ShapeDtypeStruct((t, d), F32),
        compiler_params=_cparams("parallel"),
        name="final_norm",
    )(x, g.reshape(1, d))


def _rope_tables(s, dk):
    half = dk // 2
    inv = ROPE_BASE ** (-jnp.arange(half, dtype=F32) * (2.0 / dk))
    ang = jnp.arange(s, dtype=F32)[:, None] * inv[None, :]
    return jnp.cos(ang), jnp.sin(ang)


def _tile(n, want):
    if n <= want:
        return n
    for cand in range(want, 0, -LANES):
        if n % cand == 0:
            return cand
    return n


def _even_layer(x, b, s, g, w_in, conv_w, conv_b, w_a, b_a, w_i, b_i, lam, w_out):
    t, d = x.shape
    width = w_a.shape[0] * LRU_BLOCK
    n_in = w_in.shape[1]
    proj = _norm_matmul(x, g, _pack_weight(w_in), tm=_tile(t, 1024), tn=_tile(n_in, 512))
    proj3 = proj.reshape(b, s // 2, n_in)
    ct = 256
    ya = _lru_branch(proj3, conv_w.astype(F32), conv_b.astype(F32), _lru_gate_weights(w_a, w_i, ct),
                     b_a.astype(F32), b_i.astype(F32), lam.astype(F32), width=width, ct=ct)
    cos, sin = _rope_tables(s, RET_DIM)
    log_gamma = jnp.log1p(-jnp.exp2(-5.0 - jnp.arange(RET_HEADS, dtype=F32)))
    yb = _retention_branch(proj3, cos, sin, log_gamma, col0=2 * width)
    wo = _pack_weight(w_out)
    return _proj_residual(x, [ya.reshape(t // 2, width), yb.reshape(t // 2, -1)], [(wo, 0), (wo, 1)],
                          tm=_tile(t, 512), tn=_tile(d, 512))


def _odd_layer(x, b, s, g, w_in, conv_w, conv_b, dt_bias, a_log, d_skip, norm_g, w_out):
    t, d = x.shape
    inner = w_out.shape[0]
    n_in = w_in.shape[1]
    tn = 512
    n_pad = -(-n_in // tn) * tn
    w_in_p = _pack_weight(jnp.pad(w_in.astype(F32), ((0, 0), (0, n_pad - n_in))))
    proj = _norm_matmul(x, g, w_in_p, tm=_tile(t, 1024), tn=tn)
    y = _ssd_core(proj.reshape(b, s // 2, n_pad), conv_w, conv_b, dt_bias, a_log, d_skip, norm_g, inner=inner)
    return _proj_residual(x, [y.reshape(t // 2, inner)], [(_pack_weight(w_out), 0)], tm=_tile(t, 512), tn=_tile(d, 512))


def kernel(x, mix_norm, ffn_norm, final_norm, even_w_in, lru_conv_w, lru_conv_b, lru_w_a, lru_b_a, lru_w_i,
           lru_b_i, lru_lambda, even_w_out, ssm_w_in, ssm_conv_w, ssm_conv_b, ssm_dt_bias, ssm_a_log, ssm_d,
           ssm_norm, ssm_w_out, peer_w_q, peer_sub_keys, peer_u, peer_v):
    b, s, d = x.shape
    t = b * s
    depth = mix_norm.shape[0]
    h = x.reshape(t, d).astype(F32)
    for layer in range(depth):
        j = layer // 2
        if layer % 2 == 0:
            h = _even_layer(h, b, s, mix_norm[layer], even_w_in[j], lru_conv_w[j], lru_conv_b[j], lru_w_a[j],
                            lru_b_a[j], lru_w_i[j], lru_b_i[j], lru_lambda[j], even_w_out[j])
        else:
            h = _odd_layer(h, b, s, mix_norm[layer], ssm_w_in[j], ssm_conv_w[j], ssm_conv_b[j], ssm_dt_bias[j],
                           ssm_a_log[j], ssm_d[j], ssm_norm[j], ssm_w_out[j])
        h = _peer_layer(h, ffn_norm[layer], peer_w_q[layer], peer_sub_keys[layer], peer_u[layer], peer_v[layer],
                        tm_norm=_tile(t, 512), tl=_tile(t, 512), tm=_tile(t, 1024), tn=1024, sub=256)
    return _final_norm(h, final_norm, tm=_tile(t, 512)).reshape(b, s, d).astype(x.dtype)
```

```python
import functools
import math

import jax
import jax.numpy as jnp
from jax import lax
from jax.experimental import pallas as pl
from jax.experimental.pallas import tpu as pltpu

F32 = jnp.float32
BF16 = jnp.bfloat16

EPS = 1e-6
LANES = 128
SUBLANES = 8
VMEM_LIMIT = 56 * 1024 * 1024

LRU_BLOCK = 64
LRU_C = 8.0
CONV_WIDTH = 4
RET_HEADS = 4
RET_DIM = 256
CHUNK = 128
ROPE_BASE = 10000.0
SSM_HEAD_DIM = 64
SSM_GROUPS = 4
SSM_STATE = 128
PEER_HEADS = 8
PEER_KEYS = 128
PEER_TOPK = 16
NEG_INF = float("-inf")
POS_INF = float("inf")


def _cparams(*sem):
    return pltpu.CompilerParams(dimension_semantics=sem, vmem_limit_bytes=VMEM_LIMIT)


def _softplus(x):
    return jnp.maximum(x, 0.0) + jnp.log1p(jnp.exp(-jnp.abs(x)))


def _sigmoid(x):
    return 1.0 / (1.0 + jnp.exp(-x))


def _silu(x):
    return x * _sigmoid(x)


def _gelu(x):
    c = math.sqrt(2.0 / math.pi)
    hx = 0.5 * x
    return hx + hx * jnp.tanh(x * (c + (c * 0.044715) * (x * x)))


def _pack_rows(x):
    return pltpu.bitcast(x.astype(BF16), jnp.uint32)


def _unpack_rows(x):
    return pltpu.bitcast(x, BF16)


def _rms(x, g):
    return x * lax.rsqrt(jnp.mean(x * x, axis=-1, keepdims=True) + EPS) * g


def _pack_weight_kernel(transpose, w_ref, o_ref):
    w = w_ref[...]
    o_ref[...] = _pack_rows(w.T if transpose else w)


def _pack_weight(w, *, layer=None, transpose=False, rows=512):
    r, c = w.shape[-2:]
    rt = _tile(r, rows)
    if layer is None:
        in_spec = pl.BlockSpec((rt, c), lambda i: (i, 0))
    else:
        in_spec = pl.BlockSpec((None, rt, c), lambda i: (layer, i, 0))
    if transpose:
        out_spec, out_shape = pl.BlockSpec((c // 2, rt), lambda i: (0, i)), (c // 2, r)
    else:
        out_spec, out_shape = pl.BlockSpec((rt // 2, c), lambda i: (i, 0)), (r // 2, c)
    return pl.pallas_call(
        functools.partial(_pack_weight_kernel, transpose),
        grid=(r // rt,),
        in_specs=[in_spec],
        out_specs=out_spec,
        out_shape=jax.ShapeDtypeStruct(out_shape, jnp.uint32),
        compiler_params=_cparams("parallel"),
        name="pack_weight",
    )(w.astype(F32))


def _norm_matmul_kernel(x_ref, g_ref, w_ref, o_ref, hn_ref):
    @pl.when(pl.program_id(1) == 0)
    def _():
        hn_ref[...] = _rms(x_ref[...], g_ref[...]).astype(BF16)

    o_ref[...] = _pack_rows(jnp.dot(hn_ref[...], _unpack_rows(w_ref[...]), preferred_element_type=F32))


def _norm_matmul(x, g, w, *, tm, tn):
    t, d = x.shape
    n = w.shape[1]
    return pl.pallas_call(
        _norm_matmul_kernel,
        grid=(t // tm, n // tn),
        in_specs=[pl.BlockSpec((tm, d), lambda i, j: (i, 0)),
                  pl.BlockSpec((1, d), lambda i, j: (0, 0)),
                  pl.BlockSpec((d // 2, tn), lambda i, j: (0, j))],
        out_specs=pl.BlockSpec((tm // 2, tn), lambda i, j: (i, j)),
        out_shape=jax.ShapeDtypeStruct((t // 2, n), jnp.uint32),
        scratch_shapes=[pltpu.VMEM((tm, d), BF16)],
        compiler_params=_cparams("parallel", "arbitrary"),
        name="norm_matmul",
    )(x, g.reshape(1, d), w)


def _proj_residual_kernel(n_in, x_ref, *refs):
    o_ref = refs[2 * n_in]
    acc = x_ref[...]
    for k in range(n_in):
        acc = acc + jnp.dot(_unpack_rows(refs[k][...]), _unpack_rows(refs[n_in + k][...]),
                            preferred_element_type=F32)
    o_ref[...] = acc


def _proj_residual(x, ys, ws, *, tm, tn):
    t, d = x.shape
    n_in = len(ys)
    in_specs = [pl.BlockSpec((tm, tn), lambda i, j: (i, j))]
    in_specs += [pl.BlockSpec((tm // 2, y.shape[1]), lambda i, j: (i, 0)) for y in ys]
    in_specs += [pl.BlockSpec((y.shape[1] // 2, tn), functools.partial(lambda rb, i, j: (rb, j), rb))
                 for y, (_, rb) in zip(ys, ws)]
    return pl.pallas_call(
        functools.partial(_proj_residual_kernel, n_in),
        grid=(t // tm, d // tn),
        in_specs=in_specs,
        out_specs=pl.BlockSpec((tm, tn), lambda i, j: (i, j)),
        out_shape=jax.ShapeDtypeStruct((t, d), F32),
        compiler_params=_cparams("parallel", "parallel"),
        name="proj_residual",
    )(x, *ys, *[w for w, _ in ws])


def _lru_kernel(ga_ref, xa_ref, cw_ref, cb_ref, wg_ref, ba_ref, bi_ref, lam_ref, o_ref, a_s, u_s):
    s, ct = 2 * xa_ref.shape[1], xa_ref.shape[2]
    x = _unpack_rows(xa_ref[0]).astype(F32)
    row = lax.broadcasted_iota(jnp.int32, (s, ct), 0)
    cw = cw_ref[...]
    xc = x * cw[CONV_WIDTH - 1:CONV_WIDTH, :] + cb_ref[...]
    for k in range(1, CONV_WIDTH):
        xs = jnp.where(row >= k, pltpu.roll(x, k, axis=0), 0.0)
        xc = xc + xs * cw[CONV_WIDTH - 1 - k:CONV_WIDTH - k, :]
    gates = jnp.dot(xc.astype(BF16), wg_ref[0], preferred_element_type=F32)
    r = _sigmoid(gates[:, :ct] + ba_ref[...])
    i = _sigmoid(gates[:, ct:] + bi_ref[...])
    log_a = (-LRU_C * r) * _softplus(-lam_ref[...])
    a_s[...] = jnp.exp(log_a)
    th = jnp.tanh(log_a)
    u_s[...] = jnp.sqrt(-2.0 * th / (1.0 - th)) * (i * xc)

    sub = lax.broadcasted_iota(jnp.int32, (SUBLANES, ct), 0)

    def local_scan(r0):
        a = a_s[pl.ds(r0, SUBLANES), :]
        u = u_s[pl.ds(r0, SUBLANES), :]
        for sh in (1, 2, 4):
            keep = sub >= sh
            u = jnp.where(keep, a * pltpu.roll(u, sh, axis=0) + u, u)
            a = jnp.where(keep, a * pltpu.roll(a, sh, axis=0), a)
        return a, u

    def tile_pair(t, h):
        r0 = pl.multiple_of(t * (2 * SUBLANES), 2 * SUBLANES)
        a1, u1 = local_scan(r0)
        a2, u2 = local_scan(r0 + SUBLANES)
        h1 = u1 + a1 * h
        h2 = u2 + a2 * h1[SUBLANES - 1:SUBLANES, :]
        p0 = pl.multiple_of(t * SUBLANES, SUBLANES)
        gate = _gelu(_unpack_rows(ga_ref[0, pl.ds(p0, SUBLANES), :]).astype(F32))
        o_ref[0, pl.ds(p0, SUBLANES), :] = _pack_rows(gate * jnp.concatenate([h1, h2], axis=0))
        return h2[SUBLANES - 1:SUBLANES, :]

    lax.fori_loop(0, s // (2 * SUBLANES), tile_pair, jnp.zeros((1, ct), F32), unroll=2)


def _lru_branch(proj3, conv_w, conv_b, w_gates, b_a, b_i, lam, *, width, ct):
    b, sh, _ = proj3.shape
    s = 2 * sh
    nct = width // ct
    vec = lambda: pl.BlockSpec((1, ct), lambda bi, j: (0, j))
    return pl.pallas_call(
        _lru_kernel,
        grid=(b, nct),
        in_specs=[pl.BlockSpec((1, sh, ct), lambda bi, j: (bi, 0, j)),
                  pl.BlockSpec((1, sh, ct), lambda bi, j: (bi, 0, nct + j)),
                  pl.BlockSpec((CONV_WIDTH, ct), lambda bi, j: (0, j)),
                  vec(),
                  pl.BlockSpec((1, ct, 2 * ct), lambda bi, j: (j, 0, 0)),
                  vec(), vec(), vec()],
        out_specs=pl.BlockSpec((1, sh, ct), lambda bi, j: (bi, 0, j)),
        out_shape=jax.ShapeDtypeStruct((b, sh, width), jnp.uint32),
        scratch_shapes=[pltpu.VMEM((s, ct), F32), pltpu.VMEM((s, ct), F32)],
        compiler_params=_cparams("parallel", "parallel"),
        name="rg_lru",
    )(proj3, proj3, conv_w, conv_b.reshape(1, -1), w_gates, b_a.reshape(1, -1), b_i.reshape(1, -1),
      lam.reshape(1, -1))


def _lru_gate_weights(w_a, w_i, ct):
    nb = w_a.shape[0]
    per = ct // LRU_BLOCK

    def bd(w):
        w = w.reshape(nb // per, per, LRU_BLOCK, LRU_BLOCK)
        eye = jnp.eye(per, dtype=w.dtype)
        return jnp.einsum('tpij,pq->tpiqj', w, eye).reshape(nb // per, ct, ct)

    return jnp.concatenate([bd(w_a), bd(w_i)], axis=-1).astype(BF16)


def _retention_kernel(lg_ref, q_ref, k_ref, v_ref, gb_ref, cos_ref, sin_ref, o_ref, q_s, k_s, kd_s, st_s):
    s, dk = 2 * q_ref.shape[1], q_ref.shape[2]
    half = dk // 2
    lg = lg_ref[pl.program_id(1)]
    cos = cos_ref[...]
    sin = sin_ref[...]

    def rope(x):
        x1, x2 = x[:, :half], x[:, half:]
        return jnp.concatenate([x1 * cos - x2 * sin, x1 * sin + x2 * cos], axis=-1)

    q_s[...] = (rope(_unpack_rows(q_ref[0]).astype(F32)) * (dk ** -0.5)).astype(BF16)
    k_s[...] = rope(_unpack_rows(k_ref[0]).astype(F32))
    st_s[...] = jnp.zeros_like(st_s)

    ri = lax.broadcasted_iota(jnp.int32, (CHUNK, CHUNK), 0)
    ci = lax.broadcasted_iota(jnp.int32, (CHUNK, CHUNK), 1)
    rel = (ri - ci).astype(F32)
    decay_in = jnp.where(rel >= 0, jnp.exp(lg * jnp.maximum(rel, 0.0)), 0.0)
    rowf = lax.broadcasted_iota(jnp.int32, (CHUNK, dk), 0).astype(F32)
    k_decay = jnp.exp(lg * (CHUNK - 1.0 - rowf))
    q_decay = jnp.exp(lg * (rowf + 1.0))
    chunk_decay = jnp.exp(jnp.full((1, dk), lg * CHUNK, F32))

    def chunk(c, carry):
        r0 = pl.multiple_of(c * CHUNK, CHUNK)
        qc = q_s[pl.ds(r0, CHUNK), :]
        kc = k_s[pl.ds(r0, CHUNK), :]
        p0 = pl.multiple_of(c * (CHUNK // 2), CHUNK // 2)
        vc = _unpack_rows(v_ref[0, pl.ds(p0, CHUNK // 2), :])
        scores = lax.dot_general(qc, kc.astype(BF16), (((1,), (1,)), ((), ())), preferred_element_type=F32)
        y = jnp.dot((scores * decay_in).astype(BF16), vc, preferred_element_type=F32)
        state = st_s[...]
        y = y + jnp.dot(qc, state.astype(BF16), preferred_element_type=F32) * q_decay
        kd_s[...] = (kc * k_decay).T.astype(BF16)
        st_s[...] = state * chunk_decay + jnp.dot(kd_s[...], vc, preferred_element_type=F32)
        y = y * lax.rsqrt(jnp.mean(y * y, axis=-1, keepdims=True) + EPS)
        gate = _silu(_unpack_rows(gb_ref[0, pl.ds(p0, CHUNK // 2), :]).astype(F32))
        o_ref[0, pl.ds(p0, CHUNK // 2), :] = _pack_rows(gate * y)
        return carry

    lax.fori_loop(0, s // CHUNK, chunk, 0)


def _retention_branch(proj3, cos, sin, log_gamma, *, col0):
    b, sh, _ = proj3.shape
    s = 2 * sh
    hd = RET_HEADS
    base = col0 // RET_DIM
    blk = lambda off: pl.BlockSpec((1, sh, RET_DIM), lambda bi, h: (bi, 0, base + off + h))
    tab = lambda: pl.BlockSpec((s, RET_DIM // 2), lambda bi, h: (0, 0))
    return pl.pallas_call(
        _retention_kernel,
        grid=(b, hd),
        in_specs=[pl.BlockSpec(memory_space=pltpu.SMEM),
                  blk(0), blk(hd), blk(2 * hd), blk(3 * hd), tab(), tab()],
        out_specs=pl.BlockSpec((1, sh, RET_DIM), lambda bi, h: (bi, 0, h)),
        out_shape=jax.ShapeDtypeStruct((b, sh, hd * RET_DIM), jnp.uint32),
        scratch_shapes=[pltpu.VMEM((s, RET_DIM), BF16), pltpu.VMEM((s, RET_DIM), F32),
                        pltpu.VMEM((RET_DIM, CHUNK), BF16), pltpu.VMEM((RET_DIM, RET_DIM), F32)],
        compiler_params=_cparams("parallel", "parallel"),
        name="retention",
    )(log_gamma, proj3, proj3, proj3, proj3, cos, sin)


def _ssd_kernel(z_ref, xs_ref, bm_ref, cm_ref, dt_ref, cwx_ref, cwb_ref, cwc_ref, cbx_ref, cbb_ref, cbc_ref,
                dtb_ref, aneg_ref, dexp_ref, ng_ref, exp_ref, o_ref, ex_s, eb_s, ec_s, st_s, y_s):
    inner = xs_ref.shape[2]
    gw = inner // SSM_GROUPS
    hpg = gw // SSM_HEAD_DIM

    @pl.when(pl.program_id(1) == 0)
    def _():
        ex_s[0:SUBLANES, :] = jnp.zeros((SUBLANES, ex_s.shape[1]), F32)
        eb_s[0:SUBLANES, :] = jnp.zeros((SUBLANES, eb_s.shape[1]), F32)
        ec_s[0:SUBLANES, :] = jnp.zeros((SUBLANES, ec_s.shape[1]), F32)
        st_s[...] = jnp.zeros_like(st_s)

    def conv_silu(src_ref, ext, cw_ref, cb_ref):
        ext[SUBLANES:SUBLANES + CHUNK, :] = _unpack_rows(src_ref[0]).astype(F32)
        cw = cw_ref[...]
        acc = cb_ref[...] + ext[SUBLANES:SUBLANES + CHUNK, :] * cw[CONV_WIDTH - 1:CONV_WIDTH, :]
        for k in range(1, CONV_WIDTH):
            acc = acc + ext[SUBLANES - k:SUBLANES - k + CHUNK, :] * cw[CONV_WIDTH - 1 - k:CONV_WIDTH - k, :]
        ext[0:SUBLANES, :] = ext[CHUNK:CHUNK + SUBLANES, :]
        return _silu(acc)

    xs = conv_silu(xs_ref, ex_s, cwx_ref, cbx_ref)
    bm = conv_silu(bm_ref, eb_s, cwb_ref, cbb_ref)
    cm = conv_silu(cm_ref, ec_s, cwc_ref, cbc_ref)

    dt = _softplus(_unpack_rows(dt_ref[0]).astype(F32) + dtb_ref[...])
    a = dt * aneg_ref[...]
    row = lax.broadcasted_iota(jnp.int32, (CHUNK, LANES), 0)
    a_cum = a
    sh = 1
    while sh < CHUNK:
        a_cum = a_cum + jnp.where(row >= sh, pltpu.roll(a_cum, sh, axis=0), 0.0)
        sh *= 2
    a_cum_t = a_cum.T

    expand = exp_ref[...]
    dt_e = jnp.dot(dt, expand, preferred_element_type=F32, precision=lax.Precision.HIGHEST)
    ac_e = jnp.dot(a_cum, expand, preferred_element_type=F32, precision=lax.Precision.HIGHEST)
    ac_last = ac_e[CHUNK - 1:CHUNK, :]
    xdt = xs * dt_e
    xds = (xdt * jnp.exp(ac_last - ac_e)).astype(BF16)
    ea_e = jnp.exp(ac_e)
    cdec = jnp.exp(ac_last)
    xdt_b = xdt.astype(BF16)

    li = lax.broadcasted_iota(jnp.int32, (CHUNK, CHUNK), 0)
    si = lax.broadcasted_iota(jnp.int32, (CHUNK, CHUNK), 1)
    causal = li >= si
    lane = lax.broadcasted_iota(jnp.int32, (CHUNK, LANES), 1)
    first_half = lane < SSM_HEAD_DIM

    for g in range(SSM_GROUPS):
        bg = bm[:, g * SSM_STATE:(g + 1) * SSM_STATE]
        cg = cm[:, g * SSM_STATE:(g + 1) * SSM_STATE].astype(BF16)
        bg_b = bg.astype(BF16)
        cb = lax.dot_general(cg, bg_b, (((1,), (1,)), ((), ())), preferred_element_type=F32)
        state = st_s[g]
        y_off = jnp.dot(cg, state.astype(BF16), preferred_element_type=F32) * ea_e[:, g * gw:(g + 1) * gw]
        new_states = jnp.dot(bg.T.astype(BF16), xds[:, g * gw:(g + 1) * gw], preferred_element_type=F32)
        st_s[g] = state * cdec[:, g * gw:(g + 1) * gw] + new_states
        for pr in range(hpg // 2):
            c0 = g * gw + pr * LANES
            xp = xdt_b[:, c0:c0 + LANES]
            ys = []
            for hh in (2 * pr, 2 * pr + 1):
                hd = g * hpg + hh
                seg = a_cum[:, hd:hd + 1] - a_cum_t[hd:hd + 1, :]
                lmat = jnp.exp(jnp.where(causal, seg, -1e30))
                ys.append(jnp.dot((cb * lmat).astype(BF16), xp, preferred_element_type=F32))
            y_s[:, c0:c0 + LANES] = jnp.where(first_half, ys[0], ys[1]) + y_off[:, pr * LANES:(pr + 1) * LANES]

    y = y_s[...] + dexp_ref[...] * xs
    y = y * _silu(_unpack_rows(z_ref[0]).astype(F32))
    o_ref[0] = _pack_rows(_rms(y, ng_ref[...]))


def _ssd_core(proj3, conv_w, conv_b, dt_bias, a_log, d_skip, norm_g, *, inner):
    b, sh, _ = proj3.shape
    ch = CHUNK // 2
    gn = SSM_GROUPS * SSM_STATE
    heads = inner // SSM_HEAD_DIM
    pad = LANES - heads
    dtb = jnp.pad(dt_bias.astype(F32), (0, pad)).reshape(1, LANES)
    aneg = jnp.pad(-jnp.exp(a_log.astype(F32)), (0, pad)).reshape(1, LANES)
    dexp = jnp.repeat(d_skip.astype(F32), SSM_HEAD_DIM).reshape(1, inner)
    expand = (jnp.arange(LANES)[:, None] == (jnp.arange(inner) // SSM_HEAD_DIM)[None, :]).astype(F32)
    cw = conv_w.astype(F32)
    cbias = conv_b.astype(F32).reshape(1, -1)
    full = lambda shape: pl.BlockSpec(shape, lambda bi, c: tuple(0 for _ in shape))
    return pl.pallas_call(
        _ssd_kernel,
        grid=(b, sh // ch),
        in_specs=[pl.BlockSpec((1, ch, inner), lambda bi, c: (bi, c, 0)),
                  pl.BlockSpec((1, ch, inner), lambda bi, c: (bi, c, 1)),
                  pl.BlockSpec((1, ch, gn), lambda bi, c: (bi, c, 2 * inner // gn)),
                  pl.BlockSpec((1, ch, gn), lambda bi, c: (bi, c, 2 * inner // gn + 1)),
                  pl.BlockSpec((1, ch, LANES), lambda bi, c: (bi, c, (2 * inner + 2 * gn) // LANES)),
                  full((CONV_WIDTH, inner)), full((CONV_WIDTH, gn)), full((CONV_WIDTH, gn)),
                  full((1, inner)), full((1, gn)), full((1, gn)),
                  full((1, LANES)), full((1, LANES)), full((1, inner)), full((1, inner)),
                  full((LANES, inner))],
        out_specs=pl.BlockSpec((1, ch, inner), lambda bi, c: (bi, c, 0)),
        out_shape=jax.ShapeDtypeStruct((b, sh, inner), jnp.uint32),
        scratch_shapes=[pltpu.VMEM((CHUNK + SUBLANES, inner), F32),
                        pltpu.VMEM((CHUNK + SUBLANES, gn), F32),
                        pltpu.VMEM((CHUNK + SUBLANES, gn), F32),
                        pltpu.VMEM((SSM_GROUPS, SSM_STATE, inner // SSM_GROUPS), F32),
                        pltpu.VMEM((CHUNK, inner), F32)],
        compiler_params=_cparams("parallel", "arbitrary"),
        name="ssd",
    )(proj3, proj3, proj3, proj3, proj3,
      cw[:, :inner], cw[:, inner:inner + gn], cw[:, inner + gn:],
      cbias[:, :inner], cbias[:, inner:inner + gn], cbias[:, inner + gn:],
      dtb, aneg, dexp, norm_g.astype(F32).reshape(1, inner), expand)


def _norm_t_kernel(x_ref, g_ref, o_ref):
    o_ref[...] = _pack_rows(_rms(x_ref[...], g_ref[...]).T)


def _norm_t(x, g, *, tm):
    t, d = x.shape
    return pl.pallas_call(
        _norm_t_kernel,
        grid=(t // tm,),
        in_specs=[pl.BlockSpec((tm, d), lambda i: (i, 0)), pl.BlockSpec((1, d), lambda i: (0, 0))],
        out_specs=pl.BlockSpec((d // 2, tm), lambda i: (0, i)),
        out_shape=jax.ShapeDtypeStruct((d // 2, t), jnp.uint32),
        compiler_params=_cparams("parallel"),
        name="peer_norm_t",
    )(x, g.reshape(1, d))


def _top_values(s, with_rank):
    vals = []
    cur = s
    rank = jnp.full(s.shape, float(PEER_TOPK), F32) if with_rank else None
    for r in range(PEER_TOPK):
        m = jnp.max(cur, axis=0, keepdims=True)
        vals.append(m)
        hit = cur == m
        if with_rank:
            rank = jnp.where(hit, float(r), rank)
        cur = jnp.where(hit, NEG_INF, cur)
    return jnp.concatenate(vals, axis=0), rank


def _dup_bf16(x):
    bits = pltpu.bitcast(x, jnp.uint32)
    hi = (bits + jnp.uint32(0x7FFF) + ((bits >> 16) & jnp.uint32(1))) & jnp.uint32(0xFFFF0000)
    return hi | (hi >> 16)


def _peer_select_kernel(hnt_ref, wqt_ref, keys_ref, c1_ref, w1_ref, r2_ref, w2_ref, qt_s):
    tl = hnt_ref.shape[1]
    kd = keys_ref.shape[2]
    qt_s[...] = jnp.dot(_unpack_rows(wqt_ref[...]), _unpack_rows(hnt_ref[...]),
                        preferred_element_type=F32)
    rk = lax.broadcasted_iota(jnp.int32, (PEER_TOPK, LANES), 0).astype(F32)

    def head(h, carry):
        for c in range(tl // LANES):
            ls = slice(c * LANES, (c + 1) * LANES)
            r1 = pl.multiple_of(h * (2 * kd), 2 * kd)
            q1 = qt_s[pl.ds(r1, kd), ls].astype(BF16)
            q2 = qt_s[pl.ds(r1 + kd, kd), ls].astype(BF16)
            s1 = jnp.dot(keys_ref[2 * h], q1, preferred_element_type=F32)
            s2 = jnp.dot(keys_ref[2 * h + 1], q2, preferred_element_type=F32)
            a, _ = _top_values(s1, False)
            b, rank2 = _top_values(s2, True)
            a0, b0 = a[0:1], b[0:1]
            best0 = a0 + b0
            cnt = jnp.zeros((PEER_TOPK, LANES), F32)
            cur = a + b0
            z = jnp.zeros((1, LANES), F32)
            for _ in range(PEER_TOPK):
                m = jnp.max(cur, axis=0, keepdims=True)
                idx = jnp.min(jnp.where(cur == m, rk, float(PEER_TOPK)), axis=0, keepdims=True)
                sel = rk == idx
                z = z + jnp.exp(m - best0)
                cnt = jnp.where(sel, cnt + 1.0, cnt)
                csel = jnp.max(jnp.where(sel, cnt, 0.0), axis=0, keepdims=True)
                nb = jnp.max(jnp.where(rk == csel, b, NEG_INF), axis=0, keepdims=True)
                cur = jnp.where(sel, a + nb, cur)
            cnt1 = jnp.zeros(s1.shape, F32)
            for r in range(PEER_TOPK):
                cnt1 = jnp.where(s1 == a[r:r + 1], cnt[r:r + 1], cnt1)
            c1_ref[h, :, ls] = _dup_bf16(cnt1)
            w1_ref[h, :, ls] = _dup_bf16(jnp.exp(s1 - a0))
            r2_ref[h, :, ls] = _pack_rows(rank2)
            w2_ref[h, :, ls] = _pack_rows(jnp.exp(s2 - b0) * (1.0 / z))
        return carry

    lax.fori_loop(0, PEER_HEADS, head, 0)


def _peer_select(hnt, wqt, keys, *, tl):
    dh, t = hnt.shape
    nqh = wqt.shape[0]
    nk = keys.shape[1]
    dup = jax.ShapeDtypeStruct((PEER_HEADS, nk, t), jnp.uint32)
    half = jax.ShapeDtypeStruct((PEER_HEADS, nk // 2, t), jnp.uint32)
    ospec = lambda: pl.BlockSpec((PEER_HEADS, nk, tl), lambda i: (0, 0, i))
    hspec = lambda: pl.BlockSpec((PEER_HEADS, nk // 2, tl), lambda i: (0, 0, i))
    return pl.pallas_call(
        _peer_select_kernel,
        grid=(t // tl,),
        in_specs=[pl.BlockSpec((dh, tl), lambda i: (0, i)),
                  pl.BlockSpec(wqt.shape, lambda i: (0, 0)),
                  pl.BlockSpec(keys.shape, lambda i: (0, 0, 0))],
        out_specs=[ospec(), ospec(), hspec(), hspec()],
        out_shape=[dup, dup, half, half],
        scratch_shapes=[pltpu.VMEM((2 * nqh, tl), F32)],
        compiler_params=_cparams("parallel"),
        name="peer_select",
    )(hnt, wqt, keys)


def _peer_dense_kernel(se, st, hnt_ref, u_ref, vt_ref, c1_ref, w1_ref, r2_ref, w2_ref, x_ref, zero_ref, o_ref, acc_s, ht0, ht1, act0, act1, g0, g1):
    ht_b, act_b, g_b = (ht0, ht1), (act0, act1), (g0, g1)
    tn = 2 * u_ref.shape[0]
    tm = hnt_ref.shape[1]
    j = pl.program_id(1)

    @pl.when(j == 0)
    def _():
        acc_s[...] = jnp.zeros_like(acc_s)

    per = se // PEER_KEYS
    zero = jnp.zeros((PEER_KEYS, LANES), BF16)
    blocks = [(c2, k) for c2 in range(tm // st) for k in range(tn // se)]

    def row_bf16(ref, h, i1, ls):
        words = jnp.broadcast_to(ref[h, i1:i1 + 1, ls], (PEER_KEYS // 2, LANES))
        return pltpu.bitcast(words, BF16)

    deps = {}

    def hidden(n):
        c2, k = blocks[n]
        u = _unpack_rows(u_ref[k * se // 2:(k + 1) * se // 2, :])
        hw = hnt_ref[:, c2 * st:(c2 + 1) * st]
        if n - 1 in deps:
            gate0 = hw[0:LANES, :] | jnp.tile(deps.pop(n - 1), (LANES // SUBLANES, st // LANES))
            hw = jnp.concatenate([gate0, hw[LANES:, :]], axis=0)
        ht_b[n % 2][...] = jnp.dot(u, _unpack_rows(hw), preferred_element_type=F32)

    def gates(n):
        c2, k = blocks[n]
        live = None
        for cc in range(st // LANES):
            ls = slice(c2 * st + cc * LANES, c2 * st + (cc + 1) * LANES)
            gs = [None] * per
            for h in range(PEER_HEADS):
                r2 = _unpack_rows(r2_ref[h, :, ls])
                w2 = _unpack_rows(w2_ref[h, :, ls])
                for il in range(per):
                    i1 = k * per + il
                    term = jnp.where(r2 < row_bf16(c1_ref, h, i1, ls), w2, zero) * row_bf16(w1_ref, h, i1, ls)
                    gs[il] = term if gs[il] is None else gs[il] + term
            for il in range(per):
                g_b[n % 2][il * PEER_KEYS:(il + 1) * PEER_KEYS, cc * LANES:(cc + 1) * LANES] = gs[il]
                bits = pltpu.bitcast(gs[il], jnp.uint32)
                for r in range(bits.shape[0] // SUBLANES):
                    piece = bits[r * SUBLANES:(r + 1) * SUBLANES, :]
                    live = piece if live is None else live | piece
        deps[n] = live & zero_ref[...]

    def activate(n):
        act_b[n % 2][...] = _gelu(ht_b[n % 2][...]).astype(BF16) * g_b[n % 2][...]

    def project(n):
        c2, k = blocks[n]
        vt = _unpack_rows(vt_ref[:, k * se:(k + 1) * se])
        acc_s[:, c2 * st:(c2 + 1) * st] += jnp.dot(vt, act_b[n % 2][...], preferred_element_type=F32)

    hidden(0)
    gates(0)
    for n in range(len(blocks)):
        if n + 1 < len(blocks):
            hidden(n + 1)
            gates(n + 1)
        activate(n)
        if n >= 1:
            project(n - 1)
    project(len(blocks) - 1)

    @pl.when(j == pl.num_programs(1) - 1)
    def _():
        o_ref[...] = x_ref[...] + acc_s[...].T


def _peer_dense(x, hnt, u, vt, c1, w1, r2, w2, *, tm, tn, se, st):
    t, d = x.shape
    e = 2 * u.shape[0]
    nk = c1.shape[1]
    rows = tn // PEER_KEYS
    return pl.pallas_call(
        functools.partial(_peer_dense_kernel, se, st),
        grid=(t // tm, e // tn),
        in_specs=[pl.BlockSpec((d // 2, tm), lambda i, j: (0, i), pipeline_mode=pl.Buffered(1)),
                  pl.BlockSpec((tn // 2, d), lambda i, j: (j, 0)),
                  pl.BlockSpec((d // 2, tn), lambda i, j: (0, j)),
                  pl.BlockSpec((PEER_HEADS, rows, tm), lambda i, j: (0, j, i)),
                  pl.BlockSpec((PEER_HEADS, rows, tm), lambda i, j: (0, j, i)),
                  pl.BlockSpec((PEER_HEADS, nk // 2, tm), lambda i, j: (0, 0, i), pipeline_mode=pl.Buffered(1)),
                  pl.BlockSpec((PEER_HEADS, nk // 2, tm), lambda i, j: (0, 0, i), pipeline_mode=pl.Buffered(1)),
                  pl.BlockSpec((tm, d), lambda i, j: (i, 0), pipeline_mode=pl.Buffered(1)),
                  pl.BlockSpec((SUBLANES, LANES), lambda i, j: (0, 0))],
        out_specs=pl.BlockSpec((tm, d), lambda i, j: (i, 0), pipeline_mode=pl.Buffered(1)),
        out_shape=jax.ShapeDtypeStruct((t, d), F32),
        scratch_shapes=[pltpu.VMEM((d, tm), F32), pltpu.VMEM((se, st), F32), pltpu.VMEM((se, st), F32),
                        pltpu.VMEM((se, st), BF16), pltpu.VMEM((se, st), BF16),
                        pltpu.VMEM((se, st), BF16), pltpu.VMEM((se, st), BF16)],
        compiler_params=_cparams("parallel", "arbitrary"),
        name="peer_dense",
    )(hnt, u, vt, c1, w1, r2, w2, x, jnp.zeros((SUBLANES, LANES), jnp.uint32))


def _peer_layer(x, g, layer, w_q, sub_keys, u_emb, v_emb, *, tm_norm, tl, tm, tn, se, st):
    hd, two, nk, kd = sub_keys.shape
    hnt = _norm_t(x, g, tm=tm_norm)
    c1, w1, r2, w2 = _peer_select(hnt, _pack_weight(w_q, layer=layer, transpose=True),
                                  sub_keys.reshape(hd * two, nk, kd).astype(BF16), tl=tl)
    return _peer_dense(x, hnt, _pack_weight(u_emb, layer=layer), _pack_weight(v_emb, layer=layer, transpose=True),
                       c1, w1, r2, w2, tm=tm, tn=tn, se=se, st=st)


def _final_norm_kernel(x_ref, g_ref, o_ref):
    o_ref[...] = _rms(x_ref[...], g_ref[...])


def _final_norm(x, g, *, tm):
    t, d = x.shape
    return pl.pallas_call(
        _final_norm_kernel,
        grid=(t // tm,),
        in_specs=[pl.BlockSpec((tm, d), lambda i: (i, 0)), pl.BlockSpec((1, d), lambda i: (0, 0))],
        out_specs=pl.BlockSpec((tm, d), lambda i: (i, 0)),
        out_shape=jax.ShapeDtypeStruct((t, d), F32),
        compiler_params=_cparams("parallel"),
        name="final_norm",
    )(x, g.reshape(1, d))


def _rope_tables(s, dk):
    half = dk // 2
    inv = ROPE_BASE ** (-jnp.arange(half, dtype=F32) * (2.0 / dk))
    ang = jnp.arange(s, dtype=F32)[:, None] * inv[None, :]
    return jnp.cos(ang), jnp.sin(ang)


def _tile(n, want):
    if n <= want:
        return n
    for cand in range(want, 0, -LANES):
        if n % cand == 0:
            return cand
    return n


def _even_layer(x, b, s, g, w_in, conv_w, conv_b, w_a, b_a, w_i, b_i, lam, w_out):
    t, d = x.shape
    width = w_a.shape[0] * LRU_BLOCK
    n_in = w_in.shape[1]
    proj = _norm_matmul(x, g, _pack_weight(w_in), tm=_tile(t, 1024), tn=_tile(n_in, 512))
    proj3 = proj.reshape(b, s // 2, n_in)
    ct = 256
    ya = _lru_branch(proj3, conv_w.astype(F32), conv_b.astype(F32), _lru_gate_weights(w_a, w_i, ct),
                     b_a.astype(F32), b_i.astype(F32), lam.astype(F32), width=width, ct=ct)
    cos, sin = _rope_tables(s, RET_DIM)
    log_gamma = jnp.log1p(-jnp.exp2(-5.0 - jnp.arange(RET_HEADS, dtype=F32)))
    yb = _retention_branch(proj3, cos, sin, log_gamma, col0=2 * width)
    wo = _pack_weight(w_out)
    return _proj_residual(x, [ya.reshape(t // 2, width), yb.reshape(t // 2, -1)], [(wo, 0), (wo, 1)],
                          tm=_tile(t, 512), tn=_tile(d, 512))


def _odd_layer(x, b, s, g, w_in, conv_w, conv_b, dt_bias, a_log, d_skip, norm_g, w_out):
    t, d = x.shape
    inner = w_out.shape[0]
    n_in = w_in.shape[1]
    tn = 512
    n_pad = -(-n_in // tn) * tn
    w_in_p = _pack_weight(jnp.pad(w_in.astype(F32), ((0, 0), (0, n_pad - n_in))))
    proj = _norm_matmul(x, g, w_in_p, tm=_tile(t, 1024), tn=tn)
    y = _ssd_core(proj.reshape(b, s // 2, n_pad), conv_w, conv_b, dt_bias, a_log, d_skip, norm_g, inner=inner)
    return _proj_residual(x, [y.reshape(t // 2, inner)], [(_pack_weight(w_out), 0)], tm=_tile(t, 512), tn=_tile(d, 512))


def kernel(x, mix_norm, ffn_norm, final_norm, even_w_in, lru_conv_w, lru_conv_b, lru_w_a, lru_b_a, lru_w_i,
           lru_b_i, lru_lambda, even_w_out, ssm_w_in, ssm_conv_w, ssm_conv_b, ssm_dt_bias, ssm_a_log, ssm_d,
           ssm_norm, ssm_w_out, peer_w_q, peer_sub_keys, peer_u, peer_v):
    b, s, d = x.shape
    t = b * s
    depth = mix_norm.shape[0]
    h = x.reshape(t, d).astype(F32)
    for layer in range(depth):
        j = layer // 2
        if layer % 2 == 0:
            h = _even_layer(h, b, s, mix_norm[layer], even_w_in[j], lru_conv_w[j], lru_conv_b[j], lru_w_a[j],
                            lru_b_a[j], lru_w_i[j], lru_b_i[j], lru_lambda[j], even_w_out[j])
        else:
            h = _odd_layer(h, b, s, mix_norm[layer], ssm_w_in[j], ssm_conv_w[j], ssm_conv_b[j], ssm_dt_bias[j],
                           ssm_a_log[j], ssm_d[j], ssm_norm[j], ssm_w_out[j])
        h = _peer_layer(h, ffn_norm[layer], layer, peer_w_q, peer_sub_keys[layer], peer_u, peer_v,
                        tm_norm=_tile(t, 512), tl=_tile(t, 512), tm=_tile(t, 2048), tn=1024, se=256, st=256)
    return _final_norm(h, final_norm, tm=_tile(t, 512)).reshape(b, s, d).astype(x.dtype)
```

```python
import functools
import math

import jax
import jax.numpy as jnp
from jax import lax
from jax.experimental import pallas as pl
from jax.experimental.pallas import tpu as pltpu

F32 = jnp.float32
BF16 = jnp.bfloat16

EPS = 1e-6
LANES = 128
SUBLANES = 8
VMEM_LIMIT = 56 * 1024 * 1024

LRU_BLOCK = 64
LRU_C = 8.0
CONV_WIDTH = 4
RET_HEADS = 4
RET_DIM = 256
CHUNK = 128
ROPE_BASE = 10000.0
SSM_HEAD_DIM = 64
SSM_GROUPS = 4
SSM_STATE = 128
PEER_HEADS = 8
PEER_KEYS = 128
PEER_TOPK = 16
NEG_INF = float("-inf")
POS_INF = float("inf")


def _cparams(*sem):
    return pltpu.CompilerParams(dimension_semantics=sem, vmem_limit_bytes=VMEM_LIMIT)


def _softplus(x):
    return jnp.maximum(x, 0.0) + jnp.log1p(jnp.exp(-jnp.abs(x)))


def _sigmoid(x):
    return 0.5 + 0.5 * jnp.tanh(0.5 * x)


def _silu(x):
    hx = 0.5 * x
    return hx + hx * jnp.tanh(hx)


def _gelu(x):
    c = math.sqrt(2.0 / math.pi)
    hx = 0.5 * x
    return hx + hx * jnp.tanh(x * (c + (c * 0.044715) * (x * x)))


def _pack_rows(x):
    return pltpu.bitcast(x.astype(BF16), jnp.uint32)


def _unpack_rows(x):
    return pltpu.bitcast(x, BF16)


def _rms(x, g):
    return x * lax.rsqrt(jnp.mean(x * x, axis=-1, keepdims=True) + EPS) * g


def _pack_weight_kernel(transpose, w_ref, o_ref):
    w = w_ref[...]
    o_ref[...] = _pack_rows(w.T if transpose else w)


def _pack_weight(w, *, layer=None, transpose=False, rows=512):
    r, c = w.shape[-2:]
    rt = _tile(r, rows)
    if layer is None:
        in_spec = pl.BlockSpec((rt, c), lambda i: (i, 0))
    else:
        in_spec = pl.BlockSpec((None, rt, c), lambda i: (layer, i, 0))
    if transpose:
        out_spec, out_shape = pl.BlockSpec((c // 2, rt), lambda i: (0, i)), (c // 2, r)
    else:
        out_spec, out_shape = pl.BlockSpec((rt // 2, c), lambda i: (i, 0)), (r // 2, c)
    return pl.pallas_call(
        functools.partial(_pack_weight_kernel, transpose),
        grid=(r // rt,),
        in_specs=[in_spec],
        out_specs=out_spec,
        out_shape=jax.ShapeDtypeStruct(out_shape, jnp.uint32),
        compiler_params=_cparams("parallel"),
        name="pack_weight",
    )(w.astype(F32))


def _norm_matmul_kernel(x_ref, g_ref, w_ref, o_ref, hn_ref):
    @pl.when(pl.program_id(1) == 0)
    def _():
        hn_ref[...] = _rms(x_ref[...], g_ref[...]).astype(BF16)

    o_ref[...] = _pack_rows(jnp.dot(hn_ref[...], _unpack_rows(w_ref[...]), preferred_element_type=F32))


def _norm_matmul(x, g, w, *, tm, tn):
    t, d = x.shape
    n = w.shape[1]
    return pl.pallas_call(
        _norm_matmul_kernel,
        grid=(t // tm, n // tn),
        in_specs=[pl.BlockSpec((tm, d), lambda i, j: (i, 0)),
                  pl.BlockSpec((1, d), lambda i, j: (0, 0)),
                  pl.BlockSpec((d // 2, tn), lambda i, j: (0, j))],
        out_specs=pl.BlockSpec((tm // 2, tn), lambda i, j: (i, j)),
        out_shape=jax.ShapeDtypeStruct((t // 2, n), jnp.uint32),
        scratch_shapes=[pltpu.VMEM((tm, d), BF16)],
        compiler_params=_cparams("parallel", "arbitrary"),
        name="norm_matmul",
    )(x, g.reshape(1, d), w)


def _proj_residual_kernel(n_in, x_ref, *refs):
    o_ref = refs[2 * n_in]
    acc = x_ref[...]
    for k in range(n_in):
        acc = acc + jnp.dot(_unpack_rows(refs[k][...]), _unpack_rows(refs[n_in + k][...]),
                            preferred_element_type=F32)
    o_ref[...] = acc


def _proj_residual(x, ys, ws, *, tm, tn):
    t, d = x.shape
    n_in = len(ys)
    in_specs = [pl.BlockSpec((tm, tn), lambda i, j: (i, j))]
    in_specs += [pl.BlockSpec((tm // 2, y.shape[1]), lambda i, j: (i, 0)) for y in ys]
    in_specs += [pl.BlockSpec((y.shape[1] // 2, tn), functools.partial(lambda rb, i, j: (rb, j), rb))
                 for y, (_, rb) in zip(ys, ws)]
    return pl.pallas_call(
        functools.partial(_proj_residual_kernel, n_in),
        grid=(t // tm, d // tn),
        in_specs=in_specs,
        out_specs=pl.BlockSpec((tm, tn), lambda i, j: (i, j)),
        out_shape=jax.ShapeDtypeStruct((t, d), F32),
        compiler_params=_cparams("parallel", "parallel"),
        name="proj_residual",
    )(x, *ys, *[w for w, _ in ws])


def _lru_kernel(ga_ref, xa_ref, cw_ref, cb_ref, wg_ref, ba_ref, bi_ref, lam_ref, o_ref, a_s, u_s):
    s, ct = 2 * xa_ref.shape[1], xa_ref.shape[2]
    x = _unpack_rows(xa_ref[0]).astype(F32)
    row = lax.broadcasted_iota(jnp.int32, (s, ct), 0)
    cw = cw_ref[...]
    xc = x * cw[CONV_WIDTH - 1:CONV_WIDTH, :] + cb_ref[...]
    for k in range(1, CONV_WIDTH):
        xs = jnp.where(row >= k, pltpu.roll(x, k, axis=0), 0.0)
        xc = xc + xs * cw[CONV_WIDTH - 1 - k:CONV_WIDTH - k, :]
    gates = jnp.dot(xc.astype(BF16), wg_ref[0], preferred_element_type=F32)
    r = _sigmoid(gates[:, :ct] + ba_ref[...])
    i = _sigmoid(gates[:, ct:] + bi_ref[...])
    log_a = (-LRU_C * r) * _softplus(-lam_ref[...])
    a_s[...] = jnp.exp(log_a)
    th = jnp.tanh(log_a)
    u_s[...] = jnp.sqrt(-2.0 * th / (1.0 - th)) * (i * xc)

    sub = lax.broadcasted_iota(jnp.int32, (SUBLANES, ct), 0)

    def local_scan(r0):
        a = a_s[pl.ds(r0, SUBLANES), :]
        u = u_s[pl.ds(r0, SUBLANES), :]
        for sh in (1, 2, 4):
            keep = sub >= sh
            u = jnp.where(keep, a * pltpu.roll(u, sh, axis=0) + u, u)
            a = jnp.where(keep, a * pltpu.roll(a, sh, axis=0), a)
        return a, u

    def tile_pair(t, h):
        r0 = pl.multiple_of(t * (2 * SUBLANES), 2 * SUBLANES)
        a1, u1 = local_scan(r0)
        a2, u2 = local_scan(r0 + SUBLANES)
        h1 = u1 + a1 * h
        h2 = u2 + a2 * h1[SUBLANES - 1:SUBLANES, :]
        p0 = pl.multiple_of(t * SUBLANES, SUBLANES)
        gate = _gelu(_unpack_rows(ga_ref[0, pl.ds(p0, SUBLANES), :]).astype(F32))
        o_ref[0, pl.ds(p0, SUBLANES), :] = _pack_rows(gate * jnp.concatenate([h1, h2], axis=0))
        return h2[SUBLANES - 1:SUBLANES, :]

    lax.fori_loop(0, s // (2 * SUBLANES), tile_pair, jnp.zeros((1, ct), F32), unroll=2)


def _lru_branch(proj3, conv_w, conv_b, w_gates, b_a, b_i, lam, *, width, ct):
    b, sh, _ = proj3.shape
    s = 2 * sh
    nct = width // ct
    vec = lambda: pl.BlockSpec((1, ct), lambda bi, j: (0, j))
    return pl.pallas_call(
        _lru_kernel,
        grid=(b, nct),
        in_specs=[pl.BlockSpec((1, sh, ct), lambda bi, j: (bi, 0, j)),
                  pl.BlockSpec((1, sh, ct), lambda bi, j: (bi, 0, nct + j)),
                  pl.BlockSpec((CONV_WIDTH, ct), lambda bi, j: (0, j)),
                  vec(),
                  pl.BlockSpec((1, ct, 2 * ct), lambda bi, j: (j, 0, 0)),
                  vec(), vec(), vec()],
        out_specs=pl.BlockSpec((1, sh, ct), lambda bi, j: (bi, 0, j)),
        out_shape=jax.ShapeDtypeStruct((b, sh, width), jnp.uint32),
        scratch_shapes=[pltpu.VMEM((s, ct), F32), pltpu.VMEM((s, ct), F32)],
        compiler_params=_cparams("parallel", "parallel"),
        name="rg_lru",
    )(proj3, proj3, conv_w, conv_b.reshape(1, -1), w_gates, b_a.reshape(1, -1), b_i.reshape(1, -1),
      lam.reshape(1, -1))


def _lru_gate_weights(w_a, w_i, ct):
    nb = w_a.shape[0]
    per = ct // LRU_BLOCK

    def bd(w):
        w = w.reshape(nb // per, per, LRU_BLOCK, LRU_BLOCK)
        eye = jnp.eye(per, dtype=w.dtype)
        return jnp.einsum('tpij,pq->tpiqj', w, eye).reshape(nb // per, ct, ct)

    return jnp.concatenate([bd(w_a), bd(w_i)], axis=-1).astype(BF16)


def _retention_kernel(lg_ref, q_ref, k_ref, v_ref, gb_ref, cos_ref, sin_ref, o_ref, q_s, k_s, kd_s, st_s):
    s, dk = 2 * q_ref.shape[1], q_ref.shape[2]
    half = dk // 2
    lg = lg_ref[pl.program_id(1)]
    cos = cos_ref[...]
    sin = sin_ref[...]

    def rope(x):
        x1, x2 = x[:, :half], x[:, half:]
        return jnp.concatenate([x1 * cos - x2 * sin, x1 * sin + x2 * cos], axis=-1)

    q_s[...] = (rope(_unpack_rows(q_ref[0]).astype(F32)) * (dk ** -0.5)).astype(BF16)
    k_s[...] = rope(_unpack_rows(k_ref[0]).astype(F32))
    st_s[...] = jnp.zeros_like(st_s)

    ri = lax.broadcasted_iota(jnp.int32, (CHUNK, CHUNK), 0)
    ci = lax.broadcasted_iota(jnp.int32, (CHUNK, CHUNK), 1)
    rel = (ri - ci).astype(F32)
    decay_in = jnp.where(rel >= 0, jnp.exp(lg * jnp.maximum(rel, 0.0)), 0.0)
    rowf = lax.broadcasted_iota(jnp.int32, (CHUNK, dk), 0).astype(F32)
    k_decay = jnp.exp(lg * (CHUNK - 1.0 - rowf))
    q_decay = jnp.exp(lg * (rowf + 1.0))
    chunk_decay = jnp.exp(jnp.full((1, dk), lg * CHUNK, F32))

    def chunk(c, carry):
        r0 = pl.multiple_of(c * CHUNK, CHUNK)
        qc = q_s[pl.ds(r0, CHUNK), :]
        kc = k_s[pl.ds(r0, CHUNK), :]
        p0 = pl.multiple_of(c * (CHUNK // 2), CHUNK // 2)
        vc = _unpack_rows(v_ref[0, pl.ds(p0, CHUNK // 2), :])
        scores = lax.dot_general(qc, kc.astype(BF16), (((1,), (1,)), ((), ())), preferred_element_type=F32)
        y = jnp.dot((scores * decay_in).astype(BF16), vc, preferred_element_type=F32)
        state = st_s[...]
        y = y + jnp.dot(qc, state.astype(BF16), preferred_element_type=F32) * q_decay
        kd_s[...] = (kc * k_decay).T.astype(BF16)
        st_s[...] = state * chunk_decay + jnp.dot(kd_s[...], vc, preferred_element_type=F32)
        y = y * lax.rsqrt(jnp.mean(y * y, axis=-1, keepdims=True) + EPS)
        gate = _silu(_unpack_rows(gb_ref[0, pl.ds(p0, CHUNK // 2), :]).astype(F32))
        o_ref[0, pl.ds(p0, CHUNK // 2), :] = _pack_rows(gate * y)
        return carry

    lax.fori_loop(0, s // CHUNK, chunk, 0)


def _retention_branch(proj3, cos, sin, log_gamma, *, col0):
    b, sh, _ = proj3.shape
    s = 2 * sh
    hd = RET_HEADS
    base = col0 // RET_DIM
    blk = lambda off: pl.BlockSpec((1, sh, RET_DIM), lambda bi, h: (bi, 0, base + off + h))
    tab = lambda: pl.BlockSpec((s, RET_DIM // 2), lambda bi, h: (0, 0))
    return pl.pallas_call(
        _retention_kernel,
        grid=(b, hd),
        in_specs=[pl.BlockSpec(memory_space=pltpu.SMEM),
                  blk(0), blk(hd), blk(2 * hd), blk(3 * hd), tab(), tab()],
        out_specs=pl.BlockSpec((1, sh, RET_DIM), lambda bi, h: (bi, 0, h)),
        out_shape=jax.ShapeDtypeStruct((b, sh, hd * RET_DIM), jnp.uint32),
        scratch_shapes=[pltpu.VMEM((s, RET_DIM), BF16), pltpu.VMEM((s, RET_DIM), F32),
                        pltpu.VMEM((RET_DIM, CHUNK), BF16), pltpu.VMEM((RET_DIM, RET_DIM), F32)],
        compiler_params=_cparams("parallel", "parallel"),
        name="retention",
    )(log_gamma, proj3, proj3, proj3, proj3, cos, sin)


def _ssd_kernel(z_ref, xs_ref, bm_ref, cm_ref, dt_ref, cwx_ref, cwb_ref, cwc_ref, cbx_ref, cbb_ref, cbc_ref,
                dtb_ref, aneg_ref, dexp_ref, ng_ref, exp_ref, o_ref, ex_s, eb_s, ec_s, st_s, y_s):
    inner = xs_ref.shape[2]
    gw = inner // SSM_GROUPS
    hpg = gw // SSM_HEAD_DIM

    tail = ex_s.shape[0] - CHUNK

    @pl.when(pl.program_id(1) == 0)
    def _():
        ex_s[0:tail, :] = jnp.zeros((tail, ex_s.shape[1]), BF16)
        eb_s[0:tail, :] = jnp.zeros((tail, eb_s.shape[1]), BF16)
        ec_s[0:tail, :] = jnp.zeros((tail, ec_s.shape[1]), BF16)
        st_s[...] = jnp.zeros_like(st_s)

    sr = lax.broadcasted_iota(jnp.int32, ((CONV_WIDTH - 1) * CHUNK, tail + CHUNK), 0)
    sc = lax.broadcasted_iota(jnp.int32, ((CONV_WIDTH - 1) * CHUNK, tail + CHUNK), 1)
    shift = jnp.where(sc == tail + sr % CHUNK - (sr // CHUNK + 1), 1.0, 0.0).astype(BF16)

    def conv_silu(src_ref, ext, cw_ref, cb_ref, cs):
        x = _unpack_rows(src_ref[0, :, cs])
        ext[tail:tail + CHUNK, cs] = x
        shifted = jnp.dot(shift, ext[:, cs], preferred_element_type=F32)
        cw = cw_ref[:, cs]
        acc = cb_ref[:, cs] + x.astype(F32) * cw[CONV_WIDTH - 1:CONV_WIDTH, :]
        for k in range(1, CONV_WIDTH):
            acc = acc + shifted[(k - 1) * CHUNK:k * CHUNK, :] * cw[CONV_WIDTH - 1 - k:CONV_WIDTH - k, :]
        ext[0:tail, cs] = ext[CHUNK:CHUNK + tail, cs]
        return _silu(acc)

    dt = _softplus(_unpack_rows(dt_ref[0]).astype(F32) + dtb_ref[...])
    a = dt * aneg_ref[...]
    row = lax.broadcasted_iota(jnp.int32, (CHUNK, LANES), 0)
    a_cum = a
    sh = 1
    while sh < CHUNK:
        a_cum = a_cum + jnp.where(row >= sh, pltpu.roll(a_cum, sh, axis=0), 0.0)
        sh *= 2
    a_cum_t = a_cum.T

    def top(w):
        return pltpu.bitcast(pltpu.bitcast(w, jnp.uint32) & jnp.uint32(0xFFFF0000), F32)

    def split3(v):
        hi = top(v)
        mid = top(v - hi)
        return hi.astype(BF16), mid.astype(BF16), (v - hi - mid).astype(BF16)

    def spread(parts, expand):
        return sum(jnp.dot(p, expand, preferred_element_type=F32) for p in parts)

    dt_parts = split3(dt)
    ac_parts = split3(a_cum)

    li = lax.broadcasted_iota(jnp.int32, (CHUNK, CHUNK), 0)
    si = lax.broadcasted_iota(jnp.int32, (CHUNK, CHUNK), 1)
    causal = li >= si
    lane = lax.broadcasted_iota(jnp.int32, (CHUNK, LANES), 1)
    first_half = lane < SSM_HEAD_DIM
    sumsq = jnp.zeros((CHUNK, 1), F32)

    for g in range(SSM_GROUPS):
        cs = slice(g * gw, (g + 1) * gw)
        ns = slice(g * SSM_STATE, (g + 1) * SSM_STATE)
        xs = conv_silu(xs_ref, ex_s, cwx_ref, cbx_ref, cs)
        bg = conv_silu(bm_ref, eb_s, cwb_ref, cbb_ref, ns)
        cg = conv_silu(cm_ref, ec_s, cwc_ref, cbc_ref, ns).astype(BF16)
        expand = exp_ref[:, cs]
        ac_e = spread(ac_parts, expand)
        ac_last = ac_e[CHUNK - 1:CHUNK, :]
        xdt = xs * spread(dt_parts, expand)
        xds = (xdt * jnp.exp(ac_last - ac_e)).astype(BF16)
        xdt_b = xdt.astype(BF16)
        cb = lax.dot_general(cg, bg.astype(BF16), (((1,), (1,)), ((), ())), preferred_element_type=F32)
        state = st_s[g]
        y_off = jnp.dot(cg, state.astype(BF16), preferred_element_type=F32) * jnp.exp(ac_e)
        st_s[g] = state * jnp.exp(ac_last) + jnp.dot(bg.T.astype(BF16), xds, preferred_element_type=F32)
        for pr in range(hpg // 2):
            ps = slice(pr * LANES, (pr + 1) * LANES)
            c0 = g * gw + pr * LANES
            ys = []
            for hh in (2 * pr, 2 * pr + 1):
                hd = g * hpg + hh
                seg = a_cum[:, hd:hd + 1] - a_cum_t[hd:hd + 1, :]
                lmat = jnp.exp(jnp.where(causal, seg, -1e30))
                ys.append(jnp.dot((cb * lmat).astype(BF16), xdt_b[:, ps], preferred_element_type=F32))
            y = jnp.where(first_half, ys[0], ys[1]) + y_off[:, ps] + dexp_ref[:, c0:c0 + LANES] * xs[:, ps]
            y = y * _silu(_unpack_rows(z_ref[0, :, c0:c0 + LANES]).astype(F32))
            y_s[:, c0:c0 + LANES] = y
            sumsq = sumsq + jnp.sum(y * y, axis=-1, keepdims=True)

    o_ref[0] = _pack_rows(y_s[...] * lax.rsqrt(sumsq * (1.0 / inner) + EPS) * ng_ref[...])


def _ssd_core(proj3, conv_w, conv_b, dt_bias, a_log, d_skip, norm_g, *, inner):
    b, sh, _ = proj3.shape
    ch = CHUNK // 2
    gn = SSM_GROUPS * SSM_STATE
    heads = inner // SSM_HEAD_DIM
    pad = LANES - heads
    dtb = jnp.pad(dt_bias.astype(F32), (0, pad)).reshape(1, LANES)
    aneg = jnp.pad(-jnp.exp(a_log.astype(F32)), (0, pad)).reshape(1, LANES)
    dexp = jnp.repeat(d_skip.astype(F32), SSM_HEAD_DIM).reshape(1, inner)
    expand = (jnp.arange(LANES)[:, None] == (jnp.arange(inner) // SSM_HEAD_DIM)[None, :]).astype(BF16)
    cw = conv_w.astype(F32)
    cbias = conv_b.astype(F32).reshape(1, -1)
    full = lambda shape: pl.BlockSpec(shape, lambda bi, c: tuple(0 for _ in shape))
    return pl.pallas_call(
        _ssd_kernel,
        grid=(b, sh // ch),
        in_specs=[pl.BlockSpec((1, ch, inner), lambda bi, c: (bi, c, 0)),
                  pl.BlockSpec((1, ch, inner), lambda bi, c: (bi, c, 1)),
                  pl.BlockSpec((1, ch, gn), lambda bi, c: (bi, c, 2 * inner // gn)),
                  pl.BlockSpec((1, ch, gn), lambda bi, c: (bi, c, 2 * inner // gn + 1)),
                  pl.BlockSpec((1, ch, LANES), lambda bi, c: (bi, c, (2 * inner + 2 * gn) // LANES)),
                  full((CONV_WIDTH, inner)), full((CONV_WIDTH, gn)), full((CONV_WIDTH, gn)),
                  full((1, inner)), full((1, gn)), full((1, gn)),
                  full((1, LANES)), full((1, LANES)), full((1, inner)), full((1, inner)),
                  full((LANES, inner))],
        out_specs=pl.BlockSpec((1, ch, inner), lambda bi, c: (bi, c, 0)),
        out_shape=jax.ShapeDtypeStruct((b, sh, inner), jnp.uint32),
        scratch_shapes=[pltpu.VMEM((CHUNK + 2 * SUBLANES, inner), BF16),
                        pltpu.VMEM((CHUNK + 2 * SUBLANES, gn), BF16),
                        pltpu.VMEM((CHUNK + 2 * SUBLANES, gn), BF16),
                        pltpu.VMEM((SSM_GROUPS, SSM_STATE, inner // SSM_GROUPS), F32),
                        pltpu.VMEM((CHUNK, inner), F32)],
        compiler_params=_cparams("parallel", "arbitrary"),
        name="ssd",
    )(proj3, proj3, proj3, proj3, proj3,
      cw[:, :inner], cw[:, inner:inner + gn], cw[:, inner + gn:],
      cbias[:, :inner], cbias[:, inner:inner + gn], cbias[:, inner + gn:],
      dtb, aneg, dexp, norm_g.astype(F32).reshape(1, inner), expand)


def _norm_t_kernel(x_ref, g_ref, o_ref):
    o_ref[...] = _pack_rows(_rms(x_ref[...], g_ref[...]).T)


def _norm_t(x, g, *, tm):
    t, d = x.shape
    return pl.pallas_call(
        _norm_t_kernel,
        grid=(t // tm,),
        in_specs=[pl.BlockSpec((tm, d), lambda i: (i, 0)), pl.BlockSpec((1, d), lambda i: (0, 0))],
        out_specs=pl.BlockSpec((d // 2, tm), lambda i: (0, i)),
        out_shape=jax.ShapeDtypeStruct((d // 2, t), jnp.uint32),
        compiler_params=_cparams("parallel"),
        name="peer_norm_t",
    )(x, g.reshape(1, d))


def _sort16_pairs():
    n, pairs, p = PEER_TOPK, [], 1
    while p < n:
        k = p
        while k >= 1:
            for j in range(k % p, n - k, 2 * k):
                for i in range(min(k, n - j - k)):
                    if (i + j) // (2 * p) == (i + j + k) // (2 * p):
                        pairs.append((i + j, i + j + k))
            k //= 2
        p *= 2
    return pairs


def _top_values(s):
    n = PEER_TOPK
    v = [s[i * SUBLANES:(i + 1) * SUBLANES, :] for i in range(n)]

    def exchange(x, i, j):
        x[i], x[j] = jnp.maximum(x[i], x[j]), jnp.minimum(x[i], x[j])

    for i, j in _sort16_pairs():
        exchange(v, i, j)
    shift = SUBLANES // 2
    while shift >= 1:
        v = [jnp.maximum(v[i], pltpu.roll(v[n - 1 - i], shift, axis=0)) for i in range(n)]
        j = n // 2
        while j >= 1:
            for i in range(n):
                if i ^ j > i:
                    exchange(v, i, i ^ j)
            j //= 2
        shift //= 2
    return v


def _dup_bf16(x):
    bits = pltpu.bitcast(x, jnp.uint32)
    hi = (bits + jnp.uint32(0x7FFF) + ((bits >> 16) & jnp.uint32(1))) & jnp.uint32(0xFFFF0000)
    return hi | (hi >> 16)


def _peer_select_kernel(hnt_ref, wqt_ref, keys_ref, c1_ref, w1_ref, r2_ref, w2_ref, qt_s):
    tl = hnt_ref.shape[1]
    kd = keys_ref.shape[2]
    qt_s[...] = jnp.dot(_unpack_rows(wqt_ref[...]), _unpack_rows(hnt_ref[...]),
                        preferred_element_type=F32)
    rk = lax.broadcasted_iota(jnp.int32, (PEER_TOPK, LANES), 0).astype(F32)

    def head(h, carry):
        for c in range(tl // LANES):
            ls = slice(c * LANES, (c + 1) * LANES)
            r1 = pl.multiple_of(h * (2 * kd), 2 * kd)
            q1 = qt_s[pl.ds(r1, kd), ls].astype(BF16)
            q2 = qt_s[pl.ds(r1 + kd, kd), ls].astype(BF16)
            s1 = jnp.dot(keys_ref[2 * h], q1, preferred_element_type=F32)
            s2 = jnp.dot(keys_ref[2 * h + 1], q2, preferred_element_type=F32)
            a_t = _top_values(s1)
            b_t = _top_values(s2)
            a = jnp.concatenate([t[0:1] for t in a_t], axis=0)
            b = jnp.concatenate([t[0:1] for t in b_t], axis=0)
            rep = s1.shape[0] // SUBLANES
            rank2 = jnp.zeros(s2.shape, F32)
            for r in range(PEER_TOPK):
                rank2 = jnp.where(jnp.tile(b_t[r], (rep, 1)) > s2, float(r + 1), rank2)
            a0, b0 = a[0:1], b[0:1]
            best0 = a0 + b0
            cnt = jnp.zeros((PEER_TOPK, LANES), F32)
            cur = a + b0
            z = jnp.zeros((1, LANES), F32)
            for _ in range(PEER_TOPK):
                m = jnp.max(cur, axis=0, keepdims=True)
                idx = jnp.min(jnp.where(cur == m, rk, float(PEER_TOPK)), axis=0, keepdims=True)
                sel = rk == idx
                z = z + jnp.exp(m - best0)
                cnt = jnp.where(sel, cnt + 1.0, cnt)
                csel = jnp.max(jnp.where(sel, cnt, 0.0), axis=0, keepdims=True)
                nb = jnp.max(jnp.where(rk == csel, b, NEG_INF), axis=0, keepdims=True)
                cur = jnp.where(sel, a + nb, cur)
            cnt1 = jnp.zeros(s1.shape, F32)
            for r in range(PEER_TOPK):
                cnt1 = jnp.where(s1 == jnp.tile(a_t[r], (rep, 1)), cnt[r:r + 1], cnt1)
            c1_ref[h, :, ls] = _dup_bf16(cnt1)
            w1_ref[h, :, ls] = _dup_bf16(jnp.exp(s1 - a0))
            r2_ref[h, :, ls] = _pack_rows(rank2)
            w2_ref[h, :, ls] = _pack_rows(jnp.exp(s2 - b0) * (1.0 / z))
        return carry

    lax.fori_loop(0, PEER_HEADS, head, 0)


def _peer_select(hnt, wqt, keys, *, tl):
    dh, t = hnt.shape
    nqh = wqt.shape[0]
    nk = keys.shape[1]
    dup = jax.ShapeDtypeStruct((PEER_HEADS, nk, t), jnp.uint32)
    half = jax.ShapeDtypeStruct((PEER_HEADS, nk // 2, t), jnp.uint32)
    ospec = lambda: pl.BlockSpec((PEER_HEADS, nk, tl), lambda i: (0, 0, i))
    hspec = lambda: pl.BlockSpec((PEER_HEADS, nk // 2, tl), lambda i: (0, 0, i))
    return pl.pallas_call(
        _peer_select_kernel,
        grid=(t // tl,),
        in_specs=[pl.BlockSpec((dh, tl), lambda i: (0, i)),
                  pl.BlockSpec(wqt.shape, lambda i: (0, 0)),
                  pl.BlockSpec(keys.shape, lambda i: (0, 0, 0))],
        out_specs=[ospec(), ospec(), hspec(), hspec()],
        out_shape=[dup, dup, half, half],
        scratch_shapes=[pltpu.VMEM((2 * nqh, tl), F32)],
        compiler_params=_cparams("parallel"),
        name="peer_select",
    )(hnt, wqt, keys)


def _peer_dense_kernel(se, st, hnt_ref, u_ref, vt_ref, c1_ref, w1_ref, r2_ref, w2_ref, x_ref, zero_ref, o_ref, acc_s, ht0, ht1, act0, act1, g0, g1):
    ht_b, act_b, g_b = (ht0, ht1), (act0, act1), (g0, g1)
    tn = 2 * u_ref.shape[0]
    tm = hnt_ref.shape[1]
    j = pl.program_id(1)

    @pl.when(j == 0)
    def _():
        acc_s[...] = jnp.zeros_like(acc_s)

    per = se // PEER_KEYS
    zero = jnp.zeros((PEER_KEYS, LANES), BF16)
    blocks = [(c2, k) for c2 in range(tm // st) for k in range(tn // se)]

    def row_bf16(ref, h, i1, ls):
        words = jnp.broadcast_to(ref[h, i1:i1 + 1, ls], (PEER_KEYS // 2, LANES))
        return pltpu.bitcast(words, BF16)

    deps = {}

    def hidden(n):
        c2, k = blocks[n]
        u = _unpack_rows(u_ref[k * se // 2:(k + 1) * se // 2, :])
        hw = hnt_ref[:, c2 * st:(c2 + 1) * st]
        if n - 1 in deps:
            gate0 = hw[0:LANES, :] | jnp.tile(deps.pop(n - 1), (LANES // SUBLANES, st // LANES))
            hw = jnp.concatenate([gate0, hw[LANES:, :]], axis=0)
        ht_b[n % 2][...] = jnp.dot(u, _unpack_rows(hw), preferred_element_type=F32)

    def gates(n):
        c2, k = blocks[n]
        live = None
        for cc in range(st // LANES):
            ls = slice(c2 * st + cc * LANES, c2 * st + (cc + 1) * LANES)
            gs = [None] * per
            for h in range(PEER_HEADS):
                r2 = _unpack_rows(r2_ref[h, :, ls])
                w2 = _unpack_rows(w2_ref[h, :, ls])
                for il in range(per):
                    i1 = k * per + il
                    term = jnp.where(r2 < row_bf16(c1_ref, h, i1, ls), w2, zero) * row_bf16(w1_ref, h, i1, ls)
                    gs[il] = term if gs[il] is None else gs[il] + term
            for il in range(per):
                g_b[n % 2][il * PEER_KEYS:(il + 1) * PEER_KEYS, cc * LANES:(cc + 1) * LANES] = gs[il]
                bits = pltpu.bitcast(gs[il], jnp.uint32)
                for r in range(bits.shape[0] // SUBLANES):
                    piece = bits[r * SUBLANES:(r + 1) * SUBLANES, :]
                    live = piece if live is None else live | piece
        deps[n] = live & zero_ref[...]

    def activate(n):
        act_b[n % 2][...] = _gelu(ht_b[n % 2][...]).astype(BF16) * g_b[n % 2][...]

    def project(n):
        c2, k = blocks[n]
        vt = _unpack_rows(vt_ref[:, k * se:(k + 1) * se])
        acc_s[:, c2 * st:(c2 + 1) * st] += jnp.dot(vt, act_b[n % 2][...], preferred_element_type=F32)

    hidden(0)
    gates(0)
    for n in range(len(blocks)):
        if n + 1 < len(blocks):
            hidden(n + 1)
            gates(n + 1)
        activate(n)
        if n >= 1:
            project(n - 1)
    project(len(blocks) - 1)

    @pl.when(j == pl.num_programs(1) - 1)
    def _():
        o_ref[...] = x_ref[...] + acc_s[...].T


def _peer_dense(x, hnt, u, vt, c1, w1, r2, w2, *, tm, tn, se, st):
    t, d = x.shape
    e = 2 * u.shape[0]
    nk = c1.shape[1]
    rows = tn // PEER_KEYS
    return pl.pallas_call(
        functools.partial(_peer_dense_kernel, se, st),
        grid=(t // tm, e // tn),
        in_specs=[pl.BlockSpec((d // 2, tm), lambda i, j: (0, i), pipeline_mode=pl.Buffered(1)),
                  pl.BlockSpec((tn // 2, d), lambda i, j: (j, 0)),
                  pl.BlockSpec((d // 2, tn), lambda i, j: (0, j)),
                  pl.BlockSpec((PEER_HEADS, rows, tm), lambda i, j: (0, j, i)),
                  pl.BlockSpec((PEER_HEADS, rows, tm), lambda i, j: (0, j, i)),
                  pl.BlockSpec((PEER_HEADS, nk // 2, tm), lambda i, j: (0, 0, i), pipeline_mode=pl.Buffered(1)),
                  pl.BlockSpec((PEER_HEADS, nk // 2, tm), lambda i, j: (0, 0, i), pipeline_mode=pl.Buffered(1)),
                  pl.BlockSpec((tm, d), lambda i, j: (i, 0), pipeline_mode=pl.Buffered(1)),
                  pl.BlockSpec((SUBLANES, LANES), lambda i, j: (0, 0))],
        out_specs=pl.BlockSpec((tm, d), lambda i, j: (i, 0), pipeline_mode=pl.Buffered(1)),
        out_shape=jax.ShapeDtypeStruct((t, d), F32),
        scratch_shapes=[pltpu.VMEM((d, tm), F32), pltpu.VMEM((se, st), F32), pltpu.VMEM((se, st), F32),
                        pltpu.VMEM((se, st), BF16), pltpu.VMEM((se, st), BF16),
                        pltpu.VMEM((se, st), BF16), pltpu.VMEM((se, st), BF16)],
        compiler_params=_cparams("parallel", "arbitrary"),
        name="peer_dense",
    )(hnt, u, vt, c1, w1, r2, w2, x, jnp.zeros((SUBLANES, LANES), jnp.uint32))


def _peer_layer(x, g, layer, w_q, sub_keys, u_emb, v_emb, *, tm_norm, tl, tm, tn, se, st):
    hd, two, nk, kd = sub_keys.shape
    hnt = _norm_t(x, g, tm=tm_norm)
    c1, w1, r2, w2 = _peer_select(hnt, _pack_weight(w_q, layer=layer, transpose=True),
                                  sub_keys.reshape(hd * two, nk, kd).astype(BF16), tl=tl)
    return _peer_dense(x, hnt, _pack_weight(u_emb, layer=layer), _pack_weight(v_emb, layer=layer, transpose=True),
                       c1, w1, r2, w2, tm=tm, tn=tn, se=se, st=st)


def _final_norm_kernel(x_ref, g_ref, o_ref):
    o_ref[...] = _rms(x_ref[...], g_ref[...])


def _final_norm(x, g, *, tm):
    t, d = x.shape
    return pl.pallas_call(
        _final_norm_kernel,
        grid=(t // tm,),
        in_specs=[pl.BlockSpec((tm, d), lambda i: (i, 0)), pl.BlockSpec((1, d), lambda i: (0, 0))],
        out_specs=pl.BlockSpec((tm, d), lambda i: (i, 0)),
        out_shape=jax.ShapeDtypeStruct((t, d), F32),
        compiler_params=_cparams("parallel"),
        name="final_norm",
    )(x, g.reshape(1, d))


def _rope_tables(s, dk):
    half = dk // 2
    inv = ROPE_BASE ** (-jnp.arange(half, dtype=F32) * (2.0 / dk))
    ang = jnp.arange(s, dtype=F32)[:, None] * inv[None, :]
    return jnp.cos(ang), jnp.sin(ang)


def _tile(n, want):
    if n <= want:
        return n
    for cand in range(want, 0, -LANES):
        if n % cand == 0:
            return cand
    return n


def _even_layer(x, b, s, g, w_in, conv_w, conv_b, w_a, b_a, w_i, b_i, lam, w_out):
    t, d = x.shape
    width = w_a.shape[0] * LRU_BLOCK
    n_in = w_in.shape[1]
    proj = _norm_matmul(x, g, _pack_weight(w_in), tm=_tile(t, 1024), tn=_tile(n_in, n_in // 2))
    proj3 = proj.reshape(b, s // 2, n_in)
    ct = 256
    ya = _lru_branch(proj3, conv_w.astype(F32), conv_b.astype(F32), _lru_gate_weights(w_a, w_i, ct),
                     b_a.astype(F32), b_i.astype(F32), lam.astype(F32), width=width, ct=ct)
    cos, sin = _rope_tables(s, RET_DIM)
    log_gamma = jnp.log1p(-jnp.exp2(-5.0 - jnp.arange(RET_HEADS, dtype=F32)))
    yb = _retention_branch(proj3, cos, sin, log_gamma, col0=2 * width)
    wo = _pack_weight(w_out)
    return _proj_residual(x, [ya.reshape(t // 2, width), yb.reshape(t // 2, -1)], [(wo, 0), (wo, 1)],
                          tm=_tile(t, 1024), tn=_tile(d, 1024))


def _odd_layer(x, b, s, g, w_in, conv_w, conv_b, dt_bias, a_log, d_skip, norm_g, w_out):
    t, d = x.shape
    inner = w_out.shape[0]
    n_in = w_in.shape[1]
    n_pad = -(-n_in // (2 * LANES)) * (2 * LANES)
    tn = n_pad // 2
    w_in_p = _pack_weight(jnp.pad(w_in.astype(F32), ((0, 0), (0, n_pad - n_in))))
    proj = _norm_matmul(x, g, w_in_p, tm=_tile(t, 1024), tn=tn)
    y = _ssd_core(proj.reshape(b, s // 2, n_pad), conv_w, conv_b, dt_bias, a_log, d_skip, norm_g, inner=inner)
    return _proj_residual(x, [y.reshape(t // 2, inner)], [(_pack_weight(w_out), 0)], tm=_tile(t, 1024), tn=_tile(d, 1024))


def kernel(x, mix_norm, ffn_norm, final_norm, even_w_in, lru_conv_w, lru_conv_b, lru_w_a, lru_b_a, lru_w_i,
           lru_b_i, lru_lambda, even_w_out, ssm_w_in, ssm_conv_w, ssm_conv_b, ssm_dt_bias, ssm_a_log, ssm_d,
           ssm_norm, ssm_w_out, peer_w_q, peer_sub_keys, peer_u, peer_v):
    b, s, d = x.shape
    t = b * s
    depth = mix_norm.shape[0]
    h = x.reshape(t, d).astype(F32)
    for layer in range(depth):
        j = layer // 2
        if layer % 2 == 0:
            h = _even_layer(h, b, s, mix_norm[layer], even_w_in[j], lru_conv_w[j], lru_conv_b[j], lru_w_a[j],
                            lru_b_a[j], lru_w_i[j], lru_b_i[j], lru_lambda[j], even_w_out[j])
        else:
            h = _odd_layer(h, b, s, mix_norm[layer], ssm_w_in[j], ssm_conv_w[j], ssm_conv_b[j], ssm_dt_bias[j],
                           ssm_a_log[j], ssm_d[j], ssm_norm[j], ssm_w_out[j])
        h = _peer_layer(h, ffn_norm[layer], layer, peer_w_q, peer_sub_keys[layer], peer_u, peer_v,
                        tm_norm=_tile(t, 512), tl=_tile(t, 512), tm=_tile(t, 2048), tn=1024, se=256, st=256)
    return _final_norm(h, final_norm, tm=_tile(t, 512)).reshape(b, s, d).astype(x.dtype)
```

```python
import functools
import math

import jax
import jax.numpy as jnp
from jax import lax
from jax.experimental import pallas as pl
from jax.experimental.pallas import tpu as pltpu

F32 = jnp.float32
BF16 = jnp.bfloat16

EPS = 1e-6
LANES = 128
SUBLANES = 8
VMEM_LIMIT = 56 * 1024 * 1024

LRU_BLOCK = 64
LRU_C = 8.0
CONV_WIDTH = 4
RET_HEADS = 4
RET_DIM = 256
CHUNK = 128
ROPE_BASE = 10000.0
SSM_HEAD_DIM = 64
SSM_GROUPS = 4
SSM_STATE = 128
PEER_HEADS = 8
PEER_KEYS = 128
PEER_TOPK = 16
NEG_INF = float("-inf")
POS_INF = float("inf")


def _cparams(*sem):
    return pltpu.CompilerParams(dimension_semantics=sem, vmem_limit_bytes=VMEM_LIMIT)


def _softplus(x):
    return jnp.maximum(x, 0.0) + jnp.log1p(jnp.exp(-jnp.abs(x)))


def _sigmoid(x):
    return 0.5 + 0.5 * jnp.tanh(0.5 * x)


def _silu(x):
    hx = 0.5 * x
    return hx + hx * jnp.tanh(hx)


def _gelu(x):
    c = math.sqrt(2.0 / math.pi)
    hx = 0.5 * x
    return hx + hx * jnp.tanh(x * (c + (c * 0.044715) * (x * x)))


def _pack_rows(x):
    return pltpu.bitcast(x.astype(BF16), jnp.uint32)


def _unpack_rows(x):
    return pltpu.bitcast(x, BF16)


def _rms(x, g):
    return x * lax.rsqrt(jnp.mean(x * x, axis=-1, keepdims=True) + EPS) * g


def _pack_weight_kernel(transpose, w_ref, o_ref):
    w = w_ref[...]
    o_ref[...] = _pack_rows(w.T if transpose else w)


def _pack_weight(w, *, layer=None, transpose=False, rows=512):
    r, c = w.shape[-2:]
    rt = _tile(r, rows)
    if layer is None:
        in_spec = pl.BlockSpec((rt, c), lambda i: (i, 0))
    else:
        in_spec = pl.BlockSpec((None, rt, c), lambda i: (layer, i, 0))
    if transpose:
        out_spec, out_shape = pl.BlockSpec((c // 2, rt), lambda i: (0, i)), (c // 2, r)
    else:
        out_spec, out_shape = pl.BlockSpec((rt // 2, c), lambda i: (i, 0)), (r // 2, c)
    return pl.pallas_call(
        functools.partial(_pack_weight_kernel, transpose),
        grid=(r // rt,),
        in_specs=[in_spec],
        out_specs=out_spec,
        out_shape=jax.ShapeDtypeStruct(out_shape, jnp.uint32),
        compiler_params=_cparams("parallel"),
        name="pack_weight",
    )(w.astype(F32))


def _norm_matmul_kernel(x_ref, g_ref, w_ref, o_ref, hn_ref):
    @pl.when(pl.program_id(1) == 0)
    def _():
        hn_ref[...] = _rms(x_ref[...], g_ref[...]).astype(BF16)

    o_ref[...] = _pack_rows(jnp.dot(hn_ref[...], _unpack_rows(w_ref[...]), preferred_element_type=F32))


def _norm_matmul(x, g, w, *, tm, tn):
    t, d = x.shape
    n = w.shape[1]
    return pl.pallas_call(
        _norm_matmul_kernel,
        grid=(t // tm, n // tn),
        in_specs=[pl.BlockSpec((tm, d), lambda i, j: (i, 0)),
                  pl.BlockSpec((1, d), lambda i, j: (0, 0)),
                  pl.BlockSpec((d // 2, tn), lambda i, j: (0, j))],
        out_specs=pl.BlockSpec((tm // 2, tn), lambda i, j: (i, j)),
        out_shape=jax.ShapeDtypeStruct((t // 2, n), jnp.uint32),
        scratch_shapes=[pltpu.VMEM((tm, d), BF16)],
        compiler_params=_cparams("parallel", "arbitrary"),
        name="norm_matmul",
    )(x, g.reshape(1, d), w)


def _proj_residual_kernel(n_in, x_ref, *refs):
    o_ref = refs[2 * n_in]
    acc = x_ref[...]
    for k in range(n_in):
        acc = acc + jnp.dot(_unpack_rows(refs[k][...]), _unpack_rows(refs[n_in + k][...]),
                            preferred_element_type=F32)
    o_ref[...] = acc


def _proj_residual(x, ys, ws, *, tm, tn):
    t, d = x.shape
    n_in = len(ys)
    in_specs = [pl.BlockSpec((tm, tn), lambda i, j: (i, j))]
    in_specs += [pl.BlockSpec((tm // 2, y.shape[1]), lambda i, j: (i, 0)) for y in ys]
    in_specs += [pl.BlockSpec((y.shape[1] // 2, tn), functools.partial(lambda rb, i, j: (rb, j), rb))
                 for y, (_, rb) in zip(ys, ws)]
    return pl.pallas_call(
        functools.partial(_proj_residual_kernel, n_in),
        grid=(t // tm, d // tn),
        in_specs=in_specs,
        out_specs=pl.BlockSpec((tm, tn), lambda i, j: (i, j)),
        out_shape=jax.ShapeDtypeStruct((t, d), F32),
        compiler_params=_cparams("parallel", "parallel"),
        name="proj_residual",
    )(x, *ys, *[w for w, _ in ws])


def _lru_kernel(ga_ref, xa_ref, cw_ref, cb_ref, wg_ref, ba_ref, bi_ref, lam_ref, o_ref, a_s, u_s):
    s, ct = 2 * xa_ref.shape[1], xa_ref.shape[2]
    x = _unpack_rows(xa_ref[0]).astype(F32)
    row = lax.broadcasted_iota(jnp.int32, (s, ct), 0)
    cw = cw_ref[...]
    xc = x * cw[CONV_WIDTH - 1:CONV_WIDTH, :] + cb_ref[...]
    for k in range(1, CONV_WIDTH):
        xs = jnp.where(row >= k, pltpu.roll(x, k, axis=0), 0.0)
        xc = xc + xs * cw[CONV_WIDTH - 1 - k:CONV_WIDTH - k, :]
    gates = jnp.dot(xc.astype(BF16), wg_ref[0], preferred_element_type=F32)
    r = _sigmoid(gates[:, :ct] + ba_ref[...])
    i = _sigmoid(gates[:, ct:] + bi_ref[...])
    log_a = (-LRU_C * r) * _softplus(-lam_ref[...])
    a_s[...] = jnp.exp(log_a)
    th = jnp.tanh(log_a)
    u_s[...] = jnp.sqrt(-2.0 * th / (1.0 - th)) * (i * xc)

    sub = lax.broadcasted_iota(jnp.int32, (SUBLANES, ct), 0)

    def local_scan(r0):
        a = a_s[pl.ds(r0, SUBLANES), :]
        u = u_s[pl.ds(r0, SUBLANES), :]
        for sh in (1, 2, 4):
            keep = sub >= sh
            u = jnp.where(keep, a * pltpu.roll(u, sh, axis=0) + u, u)
            a = jnp.where(keep, a * pltpu.roll(a, sh, axis=0), a)
        return a, u

    def tile_pair(t, h):
        r0 = pl.multiple_of(t * (2 * SUBLANES), 2 * SUBLANES)
        a1, u1 = local_scan(r0)
        a2, u2 = local_scan(r0 + SUBLANES)
        h1 = u1 + a1 * h
        h2 = u2 + a2 * h1[SUBLANES - 1:SUBLANES, :]
        p0 = pl.multiple_of(t * SUBLANES, SUBLANES)
        gate = _gelu(_unpack_rows(ga_ref[0, pl.ds(p0, SUBLANES), :]).astype(F32))
        o_ref[0, pl.ds(p0, SUBLANES), :] = _pack_rows(gate * jnp.concatenate([h1, h2], axis=0))
        return h2[SUBLANES - 1:SUBLANES, :]

    lax.fori_loop(0, s // (2 * SUBLANES), tile_pair, jnp.zeros((1, ct), F32), unroll=2)


def _lru_branch(proj3, conv_w, conv_b, w_gates, b_a, b_i, lam, *, width, ct):
    b, sh, _ = proj3.shape
    s = 2 * sh
    nct = width // ct
    vec = lambda: pl.BlockSpec((1, ct), lambda bi, j: (0, j))
    return pl.pallas_call(
        _lru_kernel,
        grid=(b, nct),
        in_specs=[pl.BlockSpec((1, sh, ct), lambda bi, j: (bi, 0, j)),
                  pl.BlockSpec((1, sh, ct), lambda bi, j: (bi, 0, nct + j)),
                  pl.BlockSpec((CONV_WIDTH, ct), lambda bi, j: (0, j)),
                  vec(),
                  pl.BlockSpec((1, ct, 2 * ct), lambda bi, j: (j, 0, 0)),
                  vec(), vec(), vec()],
        out_specs=pl.BlockSpec((1, sh, ct), lambda bi, j: (bi, 0, j)),
        out_shape=jax.ShapeDtypeStruct((b, sh, width), jnp.uint32),
        scratch_shapes=[pltpu.VMEM((s, ct), F32), pltpu.VMEM((s, ct), F32)],
        compiler_params=_cparams("parallel", "parallel"),
        name="rg_lru",
    )(proj3, proj3, conv_w, conv_b.reshape(1, -1), w_gates, b_a.reshape(1, -1), b_i.reshape(1, -1),
      lam.reshape(1, -1))


def _lru_gate_weights(w_a, w_i, ct):
    nb = w_a.shape[0]
    per = ct // LRU_BLOCK

    def bd(w):
        w = w.reshape(nb // per, per, LRU_BLOCK, LRU_BLOCK)
        eye = jnp.eye(per, dtype=w.dtype)
        return jnp.einsum('tpij,pq->tpiqj', w, eye).reshape(nb // per, ct, ct)

    return jnp.concatenate([bd(w_a), bd(w_i)], axis=-1).astype(BF16)


def _retention_kernel(lg_ref, q_ref, k_ref, v_ref, gb_ref, cos_ref, sin_ref, o_ref, q_s, k_s, kd_s, st_s):
    s, dk = 2 * q_ref.shape[1], q_ref.shape[2]
    half = dk // 2
    lg = lg_ref[pl.program_id(1)]
    cos = cos_ref[...]
    sin = sin_ref[...]

    def rope(x):
        x1, x2 = x[:, :half], x[:, half:]
        return jnp.concatenate([x1 * cos - x2 * sin, x1 * sin + x2 * cos], axis=-1)

    q_s[...] = (rope(_unpack_rows(q_ref[0]).astype(F32)) * (dk ** -0.5)).astype(BF16)
    k_s[...] = rope(_unpack_rows(k_ref[0]).astype(F32))
    st_s[...] = jnp.zeros_like(st_s)

    ri = lax.broadcasted_iota(jnp.int32, (CHUNK, CHUNK), 0)
    ci = lax.broadcasted_iota(jnp.int32, (CHUNK, CHUNK), 1)
    rel = (ri - ci).astype(F32)
    decay_in = jnp.where(rel >= 0, jnp.exp(lg * jnp.maximum(rel, 0.0)), 0.0)
    rowf = lax.broadcasted_iota(jnp.int32, (CHUNK, dk), 0).astype(F32)
    k_decay = jnp.exp(lg * (CHUNK - 1.0 - rowf))
    q_decay = jnp.exp(lg * (rowf + 1.0))
    chunk_decay = jnp.exp(jnp.full((1, dk), lg * CHUNK, F32))

    def chunk(c, carry):
        r0 = pl.multiple_of(c * CHUNK, CHUNK)
        qc = q_s[pl.ds(r0, CHUNK), :]
        kc = k_s[pl.ds(r0, CHUNK), :]
        p0 = pl.multiple_of(c * (CHUNK // 2), CHUNK // 2)
        vc = _unpack_rows(v_ref[0, pl.ds(p0, CHUNK // 2), :])
        scores = lax.dot_general(qc, kc.astype(BF16), (((1,), (1,)), ((), ())), preferred_element_type=F32)
        y = jnp.dot((scores * decay_in).astype(BF16), vc, preferred_element_type=F32)
        state = st_s[...]
        y = y + jnp.dot(qc, state.astype(BF16), preferred_element_type=F32) * q_decay
        kd_s[...] = (kc * k_decay).T.astype(BF16)
        st_s[...] = state * chunk_decay + jnp.dot(kd_s[...], vc, preferred_element_type=F32)
        y = y * lax.rsqrt(jnp.mean(y * y, axis=-1, keepdims=True) + EPS)
        gate = _silu(_unpack_rows(gb_ref[0, pl.ds(p0, CHUNK // 2), :]).astype(F32))
        o_ref[0, pl.ds(p0, CHUNK // 2), :] = _pack_rows(gate * y)
        return carry

    lax.fori_loop(0, s // CHUNK, chunk, 0, unroll=4)


def _retention_branch(proj3, cos, sin, log_gamma, *, col0):
    b, sh, _ = proj3.shape
    s = 2 * sh
    hd = RET_HEADS
    base = col0 // RET_DIM
    blk = lambda off: pl.BlockSpec((1, sh, RET_DIM), lambda bi, h: (bi, 0, base + off + h))
    tab = lambda: pl.BlockSpec((s, RET_DIM // 2), lambda bi, h: (0, 0))
    return pl.pallas_call(
        _retention_kernel,
        grid=(b, hd),
        in_specs=[pl.BlockSpec(memory_space=pltpu.SMEM),
                  blk(0), blk(hd), blk(2 * hd), blk(3 * hd), tab(), tab()],
        out_specs=pl.BlockSpec((1, sh, RET_DIM), lambda bi, h: (bi, 0, h)),
        out_shape=jax.ShapeDtypeStruct((b, sh, hd * RET_DIM), jnp.uint32),
        scratch_shapes=[pltpu.VMEM((s, RET_DIM), BF16), pltpu.VMEM((s, RET_DIM), F32),
                        pltpu.VMEM((RET_DIM, CHUNK), BF16), pltpu.VMEM((RET_DIM, RET_DIM), F32)],
        compiler_params=_cparams("parallel", "parallel"),
        name="retention",
    )(log_gamma, proj3, proj3, proj3, proj3, cos, sin)


def _ssd_kernel(z_ref, xs_ref, bm_ref, cm_ref, dt_ref, cwx_ref, cwb_ref, cwc_ref, cbx_ref, cbb_ref, cbc_ref,
                dtb_ref, aneg_ref, dexp_ref, ng_ref, exp_ref, o_ref, ex_s, eb_s, ec_s, st_s, y_s):
    inner = xs_ref.shape[2]
    gw = inner // SSM_GROUPS
    hpg = gw // SSM_HEAD_DIM

    tail = ex_s.shape[0] - CHUNK

    @pl.when(pl.program_id(1) == 0)
    def _():
        ex_s[0:tail, :] = jnp.zeros((tail, ex_s.shape[1]), BF16)
        eb_s[0:tail, :] = jnp.zeros((tail, eb_s.shape[1]), BF16)
        ec_s[0:tail, :] = jnp.zeros((tail, ec_s.shape[1]), BF16)
        st_s[...] = jnp.zeros_like(st_s)

    sr = lax.broadcasted_iota(jnp.int32, ((CONV_WIDTH - 1) * CHUNK, tail + CHUNK), 0)
    sc = lax.broadcasted_iota(jnp.int32, ((CONV_WIDTH - 1) * CHUNK, tail + CHUNK), 1)
    shift = jnp.where(sc == tail + sr % CHUNK - (sr // CHUNK + 1), 1.0, 0.0).astype(BF16)

    def conv_silu(src_ref, ext, cw_ref, cb_ref, cs):
        x = _unpack_rows(src_ref[0, :, cs])
        ext[tail:tail + CHUNK, cs] = x
        shifted = jnp.dot(shift, ext[:, cs], preferred_element_type=F32)
        cw = cw_ref[:, cs]
        acc = cb_ref[:, cs] + x.astype(F32) * cw[CONV_WIDTH - 1:CONV_WIDTH, :]
        for k in range(1, CONV_WIDTH):
            acc = acc + shifted[(k - 1) * CHUNK:k * CHUNK, :] * cw[CONV_WIDTH - 1 - k:CONV_WIDTH - k, :]
        ext[0:tail, cs] = ext[CHUNK:CHUNK + tail, cs]
        return _silu(acc)

    dt = _softplus(_unpack_rows(dt_ref[0]).astype(F32) + dtb_ref[...])
    a = dt * aneg_ref[...]
    row = lax.broadcasted_iota(jnp.int32, (CHUNK, LANES), 0)
    a_cum = a
    sh = 1
    while sh < CHUNK:
        a_cum = a_cum + jnp.where(row >= sh, pltpu.roll(a_cum, sh, axis=0), 0.0)
        sh *= 2
    a_cum_t = a_cum.T

    def top(w):
        return pltpu.bitcast(pltpu.bitcast(w, jnp.uint32) & jnp.uint32(0xFFFF0000), F32)

    def split3(v):
        hi = top(v)
        mid = top(v - hi)
        return hi.astype(BF16), mid.astype(BF16), (v - hi - mid).astype(BF16)

    def spread(parts, expand):
        return sum(jnp.dot(p, expand, preferred_element_type=F32) for p in parts)

    dt_parts = split3(dt)
    ac_parts = split3(a_cum)

    li = lax.broadcasted_iota(jnp.int32, (CHUNK, CHUNK), 0)
    si = lax.broadcasted_iota(jnp.int32, (CHUNK, CHUNK), 1)
    causal = li >= si
    lane = lax.broadcasted_iota(jnp.int32, (CHUNK, LANES), 1)
    first_half = lane < SSM_HEAD_DIM
    sumsq = jnp.zeros((CHUNK, 1), F32)

    for g in range(SSM_GROUPS):
        cs = slice(g * gw, (g + 1) * gw)
        ns = slice(g * SSM_STATE, (g + 1) * SSM_STATE)
        xs = conv_silu(xs_ref, ex_s, cwx_ref, cbx_ref, cs)
        bg = conv_silu(bm_ref, eb_s, cwb_ref, cbb_ref, ns)
        cg = conv_silu(cm_ref, ec_s, cwc_ref, cbc_ref, ns).astype(BF16)
        expand = exp_ref[:, cs]
        ac_e = spread(ac_parts, expand)
        ac_last = ac_e[CHUNK - 1:CHUNK, :]
        xdt = xs * spread(dt_parts, expand)
        xds = (xdt * jnp.exp(ac_last - ac_e)).astype(BF16)
        xdt_b = xdt.astype(BF16)
        cb = lax.dot_general(cg, bg.astype(BF16), (((1,), (1,)), ((), ())), preferred_element_type=F32)
        state = st_s[g]
        y_off = jnp.dot(cg, state.astype(BF16), preferred_element_type=F32) * jnp.exp(ac_e)
        st_s[g] = state * jnp.exp(ac_last) + jnp.dot(bg.T.astype(BF16), xds, preferred_element_type=F32)
        for pr in range(hpg // 2):
            ps = slice(pr * LANES, (pr + 1) * LANES)
            c0 = g * gw + pr * LANES
            ys = []
            for hh in (2 * pr, 2 * pr + 1):
                hd = g * hpg + hh
                seg = a_cum[:, hd:hd + 1] - a_cum_t[hd:hd + 1, :]
                lmat = jnp.exp(jnp.where(causal, seg, -1e30))
                ys.append(jnp.dot((cb * lmat).astype(BF16), xdt_b[:, ps], preferred_element_type=F32))
            y = jnp.where(first_half, ys[0], ys[1]) + y_off[:, ps] + dexp_ref[:, c0:c0 + LANES] * xs[:, ps]
            y = y * _silu(_unpack_rows(z_ref[0, :, c0:c0 + LANES]).astype(F32))
            y_s[:, c0:c0 + LANES] = y
            sumsq = sumsq + jnp.sum(y * y, axis=-1, keepdims=True)

    o_ref[0] = _pack_rows(y_s[...] * lax.rsqrt(sumsq * (1.0 / inner) + EPS) * ng_ref[...])


def _ssd_core(proj3, conv_w, conv_b, dt_bias, a_log, d_skip, norm_g, *, inner):
    b, sh, _ = proj3.shape
    ch = CHUNK // 2
    gn = SSM_GROUPS * SSM_STATE
    heads = inner // SSM_HEAD_DIM
    pad = LANES - heads
    dtb = jnp.pad(dt_bias.astype(F32), (0, pad)).reshape(1, LANES)
    aneg = jnp.pad(-jnp.exp(a_log.astype(F32)), (0, pad)).reshape(1, LANES)
    dexp = jnp.repeat(d_skip.astype(F32), SSM_HEAD_DIM).reshape(1, inner)
    expand = (jnp.arange(LANES)[:, None] == (jnp.arange(inner) // SSM_HEAD_DIM)[None, :]).astype(BF16)
    cw = conv_w.astype(F32)
    cbias = conv_b.astype(F32).reshape(1, -1)
    full = lambda shape: pl.BlockSpec(shape, lambda bi, c: tuple(0 for _ in shape))
    return pl.pallas_call(
        _ssd_kernel,
        grid=(b, sh // ch),
        in_specs=[pl.BlockSpec((1, ch, inner), lambda bi, c: (bi, c, 0)),
                  pl.BlockSpec((1, ch, inner), lambda bi, c: (bi, c, 1)),
                  pl.BlockSpec((1, ch, gn), lambda bi, c: (bi, c, 2 * inner // gn)),
                  pl.BlockSpec((1, ch, gn), lambda bi, c: (bi, c, 2 * inner // gn + 1)),
                  pl.BlockSpec((1, ch, LANES), lambda bi, c: (bi, c, (2 * inner + 2 * gn) // LANES)),
                  full((CONV_WIDTH, inner)), full((CONV_WIDTH, gn)), full((CONV_WIDTH, gn)),
                  full((1, inner)), full((1, gn)), full((1, gn)),
                  full((1, LANES)), full((1, LANES)), full((1, inner)), full((1, inner)),
                  full((LANES, inner))],
        out_specs=pl.BlockSpec((1, ch, inner), lambda bi, c: (bi, c, 0)),
        out_shape=jax.ShapeDtypeStruct((b, sh, inner), jnp.uint32),
        scratch_shapes=[pltpu.VMEM((CHUNK + 2 * SUBLANES, inner), BF16),
                        pltpu.VMEM((CHUNK + 2 * SUBLANES, gn), BF16),
                        pltpu.VMEM((CHUNK + 2 * SUBLANES, gn), BF16),
                        pltpu.VMEM((SSM_GROUPS, SSM_STATE, inner // SSM_GROUPS), F32),
                        pltpu.VMEM((CHUNK, inner), F32)],
        compiler_params=_cparams("parallel", "arbitrary"),
        name="ssd",
    )(proj3, proj3, proj3, proj3, proj3,
      cw[:, :inner], cw[:, inner:inner + gn], cw[:, inner + gn:],
      cbias[:, :inner], cbias[:, inner:inner + gn], cbias[:, inner + gn:],
      dtb, aneg, dexp, norm_g.astype(F32).reshape(1, inner), expand)


def _norm_t_kernel(x_ref, g_ref, o_ref):
    o_ref[...] = _pack_rows(_rms(x_ref[...], g_ref[...]).T)


def _norm_t(x, g, *, tm):
    t, d = x.shape
    return pl.pallas_call(
        _norm_t_kernel,
        grid=(t // tm,),
        in_specs=[pl.BlockSpec((tm, d), lambda i: (i, 0)), pl.BlockSpec((1, d), lambda i: (0, 0))],
        out_specs=pl.BlockSpec((d // 2, tm), lambda i: (0, i)),
        out_shape=jax.ShapeDtypeStruct((d // 2, t), jnp.uint32),
        compiler_params=_cparams("parallel"),
        name="peer_norm_t",
    )(x, g.reshape(1, d))


def _sort16_pairs():
    n, pairs, p = PEER_TOPK, [], 1
    while p < n:
        k = p
        while k >= 1:
            for j in range(k % p, n - k, 2 * k):
                for i in range(min(k, n - j - k)):
                    if (i + j) // (2 * p) == (i + j + k) // (2 * p):
                        pairs.append((i + j, i + j + k))
            k //= 2
        p *= 2
    return pairs


def _top_values(s):
    n = PEER_TOPK
    v = [s[i * SUBLANES:(i + 1) * SUBLANES, :] for i in range(n)]

    def exchange(x, i, j):
        x[i], x[j] = jnp.maximum(x[i], x[j]), jnp.minimum(x[i], x[j])

    for i, j in _sort16_pairs():
        exchange(v, i, j)
    shift = SUBLANES // 2
    while shift >= 1:
        v = [jnp.maximum(v[i], pltpu.roll(v[n - 1 - i], shift, axis=0)) for i in range(n)]
        j = n // 2
        while j >= 1:
            for i in range(n):
                if i ^ j > i:
                    exchange(v, i, i ^ j)
            j //= 2
        shift //= 2
    return v


def _dup_bf16(x):
    bits = pltpu.bitcast(x, jnp.uint32)
    hi = (bits + jnp.uint32(0x7FFF) + ((bits >> 16) & jnp.uint32(1))) & jnp.uint32(0xFFFF0000)
    return hi | (hi >> 16)


def _peer_select_kernel(hnt_ref, wqt_ref, keys_ref, c1_ref, w1_ref, r2_ref, w2_ref, qt_s):
    tl = hnt_ref.shape[1]
    kd = keys_ref.shape[2]
    qt_s[...] = jnp.dot(_unpack_rows(wqt_ref[...]), _unpack_rows(hnt_ref[...]),
                        preferred_element_type=F32)
    rk = lax.broadcasted_iota(jnp.int32, (PEER_TOPK, LANES), 0).astype(F32)

    def head(h, carry):
        for c in range(tl // LANES):
            ls = slice(c * LANES, (c + 1) * LANES)
            r1 = pl.multiple_of(h * (2 * kd), 2 * kd)
            q1 = qt_s[pl.ds(r1, kd), ls].astype(BF16)
            q2 = qt_s[pl.ds(r1 + kd, kd), ls].astype(BF16)
            s1 = jnp.dot(keys_ref[2 * h], q1, preferred_element_type=F32)
            s2 = jnp.dot(keys_ref[2 * h + 1], q2, preferred_element_type=F32)
            a_t = _top_values(s1)
            b_t = _top_values(s2)
            a = jnp.concatenate([t[0:1] for t in a_t], axis=0)
            b = jnp.concatenate([t[0:1] for t in b_t], axis=0)
            rep = s1.shape[0] // SUBLANES
            rank2 = jnp.zeros(s2.shape, F32)
            for r in range(PEER_TOPK):
                rank2 = jnp.where(jnp.tile(b_t[r], (rep, 1)) > s2, float(r + 1), rank2)
            a0, b0 = a[0:1], b[0:1]
            best0 = a0 + b0
            cnt = jnp.zeros((PEER_TOPK, LANES), F32)
            cur = a + b0
            z = jnp.zeros((1, LANES), F32)
            for _ in range(PEER_TOPK):
                m = jnp.max(cur, axis=0, keepdims=True)
                idx = jnp.min(jnp.where(cur == m, rk, float(PEER_TOPK)), axis=0, keepdims=True)
                sel = rk == idx
                z = z + jnp.exp(m - best0)
                cnt = jnp.where(sel, cnt + 1.0, cnt)
                csel = jnp.max(jnp.where(sel, cnt, 0.0), axis=0, keepdims=True)
                nb = jnp.max(jnp.where(rk == csel, b, NEG_INF), axis=0, keepdims=True)
                cur = jnp.where(sel, a + nb, cur)
            cnt1 = jnp.zeros(s1.shape, F32)
            for r in range(PEER_TOPK):
                cnt1 = jnp.where(s1 == jnp.tile(a_t[r], (rep, 1)), cnt[r:r + 1], cnt1)
            c1_ref[h, :, ls] = _dup_bf16(cnt1)
            w1_ref[h, :, ls] = _dup_bf16(jnp.exp(s1 - a0))
            r2_ref[h, :, ls] = _pack_rows(rank2)
            w2_ref[h, :, ls] = _pack_rows(jnp.exp(s2 - b0) * (1.0 / z))
        return carry

    lax.fori_loop(0, PEER_HEADS, head, 0, unroll=4)


def _peer_select(hnt, wqt, keys, *, tl):
    dh, t = hnt.shape
    nqh = wqt.shape[0]
    nk = keys.shape[1]
    dup = jax.ShapeDtypeStruct((PEER_HEADS, nk, t), jnp.uint32)
    half = jax.ShapeDtypeStruct((PEER_HEADS, nk // 2, t), jnp.uint32)
    ospec = lambda: pl.BlockSpec((PEER_HEADS, nk, tl), lambda i: (0, 0, i))
    hspec = lambda: pl.BlockSpec((PEER_HEADS, nk // 2, tl), lambda i: (0, 0, i))
    return pl.pallas_call(
        _peer_select_kernel,
        grid=(t // tl,),
        in_specs=[pl.BlockSpec((dh, tl), lambda i: (0, i)),
                  pl.BlockSpec(wqt.shape, lambda i: (0, 0)),
                  pl.BlockSpec(keys.shape, lambda i: (0, 0, 0))],
        out_specs=[ospec(), ospec(), hspec(), hspec()],
        out_shape=[dup, dup, half, half],
        scratch_shapes=[pltpu.VMEM((2 * nqh, tl), F32)],
        compiler_params=_cparams("parallel"),
        name="peer_select",
    )(hnt, wqt, keys)


def _peer_dense_kernel(se, st, final, hnt_ref, u_ref, vt_ref, c1_ref, w1_ref, r2_ref, w2_ref, x_ref, zero_ref, fg_ref, o_ref, acc_s, ht0, ht1, act0, act1, g0, g1):
    ht_b, act_b, g_b = (ht0, ht1), (act0, act1), (g0, g1)
    tn = 2 * u_ref.shape[0]
    tm = hnt_ref.shape[1]
    j = pl.program_id(1)

    @pl.when(j == 0)
    def _():
        acc_s[...] = jnp.zeros_like(acc_s)

    per = se // PEER_KEYS
    zero = jnp.zeros((PEER_KEYS, LANES), BF16)
    blocks = [(c2, k) for c2 in range(tm // st) for k in range(tn // se)]

    def row_bf16(ref, h, i1, ls):
        words = jnp.broadcast_to(ref[h, i1:i1 + 1, ls], (PEER_KEYS // 2, LANES))
        return pltpu.bitcast(words, BF16)

    deps = {}

    def hidden(n):
        c2, k = blocks[n]
        u = _unpack_rows(u_ref[k * se // 2:(k + 1) * se // 2, :])
        hw = hnt_ref[:, c2 * st:(c2 + 1) * st]
        if n - 1 in deps:
            gate0 = hw[0:LANES, :] | jnp.tile(deps.pop(n - 1), (LANES // SUBLANES, st // LANES))
            hw = jnp.concatenate([gate0, hw[LANES:, :]], axis=0)
        ht_b[n % 2][...] = jnp.dot(u, _unpack_rows(hw), preferred_element_type=F32)

    def gates(n):
        c2, k = blocks[n]
        live = None
        for cc in range(st // LANES):
            ls = slice(c2 * st + cc * LANES, c2 * st + (cc + 1) * LANES)
            gs = [None] * per
            for h in range(PEER_HEADS):
                r2 = _unpack_rows(r2_ref[h, :, ls])
                w2 = _unpack_rows(w2_ref[h, :, ls])
                for il in range(per):
                    i1 = k * per + il
                    term = jnp.where(r2 < row_bf16(c1_ref, h, i1, ls), w2, zero) * row_bf16(w1_ref, h, i1, ls)
                    gs[il] = term if gs[il] is None else gs[il] + term
            for il in range(per):
                g_b[n % 2][il * PEER_KEYS:(il + 1) * PEER_KEYS, cc * LANES:(cc + 1) * LANES] = gs[il]
                bits = pltpu.bitcast(gs[il], jnp.uint32)
                for r in range(bits.shape[0] // SUBLANES):
                    piece = bits[r * SUBLANES:(r + 1) * SUBLANES, :]
                    live = piece if live is None else live | piece
        deps[n] = live & zero_ref[...]

    def activate(n):
        act_b[n % 2][...] = _gelu(ht_b[n % 2][...]).astype(BF16) * g_b[n % 2][...]

    def project(n):
        c2, k = blocks[n]
        vt = _unpack_rows(vt_ref[:, k * se:(k + 1) * se])
        acc_s[:, c2 * st:(c2 + 1) * st] += jnp.dot(vt, act_b[n % 2][...], preferred_element_type=F32)

    hidden(0)
    gates(0)
    for n in range(len(blocks)):
        if n + 1 < len(blocks):
            hidden(n + 1)
            gates(n + 1)
        activate(n)
        if n >= 1:
            project(n - 1)
    project(len(blocks) - 1)

    @pl.when(j == pl.num_programs(1) - 1)
    def _():
        y = x_ref[...] + acc_s[...].T
        o_ref[...] = _rms(y, fg_ref[...]) if final else y


def _peer_dense(x, hnt, u, vt, c1, w1, r2, w2, final_g, *, tm, tn, se, st):
    t, d = x.shape
    e = 2 * u.shape[0]
    nk = c1.shape[1]
    rows = tn // PEER_KEYS
    return pl.pallas_call(
        functools.partial(_peer_dense_kernel, se, st, final_g is not None),
        grid=(t // tm, e // tn),
        in_specs=[pl.BlockSpec((d // 2, tm), lambda i, j: (0, i), pipeline_mode=pl.Buffered(1)),
                  pl.BlockSpec((tn // 2, d), lambda i, j: (j, 0)),
                  pl.BlockSpec((d // 2, tn), lambda i, j: (0, j)),
                  pl.BlockSpec((PEER_HEADS, rows, tm), lambda i, j: (0, j, i)),
                  pl.BlockSpec((PEER_HEADS, rows, tm), lambda i, j: (0, j, i)),
                  pl.BlockSpec((PEER_HEADS, nk // 2, tm), lambda i, j: (0, 0, i), pipeline_mode=pl.Buffered(1)),
                  pl.BlockSpec((PEER_HEADS, nk // 2, tm), lambda i, j: (0, 0, i), pipeline_mode=pl.Buffered(1)),
                  pl.BlockSpec((tm, d), lambda i, j: (i, 0), pipeline_mode=pl.Buffered(1)),
                  pl.BlockSpec((SUBLANES, LANES), lambda i, j: (0, 0)),
                  pl.BlockSpec((1, d), lambda i, j: (0, 0))],
        out_specs=pl.BlockSpec((tm, d), lambda i, j: (i, 0), pipeline_mode=pl.Buffered(1)),
        out_shape=jax.ShapeDtypeStruct((t, d), F32),
        scratch_shapes=[pltpu.VMEM((d, tm), F32), pltpu.VMEM((se, st), F32), pltpu.VMEM((se, st), F32),
                        pltpu.VMEM((se, st), BF16), pltpu.VMEM((se, st), BF16),
                        pltpu.VMEM((se, st), BF16), pltpu.VMEM((se, st), BF16)],
        compiler_params=_cparams("parallel", "arbitrary"),
        name="peer_dense",
    )(hnt, u, vt, c1, w1, r2, w2, x, jnp.zeros((SUBLANES, LANES), jnp.uint32),
      (jnp.ones((d,), F32) if final_g is None else final_g.astype(F32)).reshape(1, d))


def _peer_layer(x, g, layer, w_q, sub_keys, u_emb, v_emb, final_g, *, tm_norm, tl, tm, tn, se, st):
    hd, two, nk, kd = sub_keys.shape
    hnt = _norm_t(x, g, tm=tm_norm)
    c1, w1, r2, w2 = _peer_select(hnt, _pack_weight(w_q, layer=layer, transpose=True),
                                  sub_keys.reshape(hd * two, nk, kd).astype(BF16), tl=tl)
    return _peer_dense(x, hnt, _pack_weight(u_emb, layer=layer), _pack_weight(v_emb, layer=layer, transpose=True),
                       c1, w1, r2, w2, final_g, tm=tm, tn=tn, se=se, st=st)


def _rope_tables(s, dk):
    half = dk // 2
    inv = ROPE_BASE ** (-jnp.arange(half, dtype=F32) * (2.0 / dk))
    ang = jnp.arange(s, dtype=F32)[:, None] * inv[None, :]
    return jnp.cos(ang), jnp.sin(ang)


def _tile(n, want):
    if n <= want:
        return n
    for cand in range(want, 0, -LANES):
        if n % cand == 0:
            return cand
    return n


def _even_layer(x, b, s, g, w_in, conv_w, conv_b, w_a, b_a, w_i, b_i, lam, w_out):
    t, d = x.shape
    width = w_a.shape[0] * LRU_BLOCK
    n_in = w_in.shape[1]
    proj = _norm_matmul(x, g, _pack_weight(w_in), tm=_tile(t, 1024), tn=_tile(n_in, n_in // 2))
    proj3 = proj.reshape(b, s // 2, n_in)
    ct = 256
    ya = _lru_branch(proj3, conv_w.astype(F32), conv_b.astype(F32), _lru_gate_weights(w_a, w_i, ct),
                     b_a.astype(F32), b_i.astype(F32), lam.astype(F32), width=width, ct=ct)
    cos, sin = _rope_tables(s, RET_DIM)
    log_gamma = jnp.log1p(-jnp.exp2(-5.0 - jnp.arange(RET_HEADS, dtype=F32)))
    yb = _retention_branch(proj3, cos, sin, log_gamma, col0=2 * width)
    wo = _pack_weight(w_out)
    return _proj_residual(x, [ya.reshape(t // 2, width), yb.reshape(t // 2, -1)], [(wo, 0), (wo, 1)],
                          tm=_tile(t, 1024), tn=_tile(d, 1024))


def _odd_layer(x, b, s, g, w_in, conv_w, conv_b, dt_bias, a_log, d_skip, norm_g, w_out):
    t, d = x.shape
    inner = w_out.shape[0]
    n_in = w_in.shape[1]
    n_pad = -(-n_in // (2 * LANES)) * (2 * LANES)
    tn = n_pad // 2
    w_in_p = _pack_weight(jnp.pad(w_in.astype(F32), ((0, 0), (0, n_pad - n_in))))
    proj = _norm_matmul(x, g, w_in_p, tm=_tile(t, 1024), tn=tn)
    y = _ssd_core(proj.reshape(b, s // 2, n_pad), conv_w, conv_b, dt_bias, a_log, d_skip, norm_g, inner=inner)
    return _proj_residual(x, [y.reshape(t // 2, inner)], [(_pack_weight(w_out), 0)], tm=_tile(t, 1024), tn=_tile(d, 1024))


def kernel(x, mix_norm, ffn_norm, final_norm, even_w_in, lru_conv_w, lru_conv_b, lru_w_a, lru_b_a, lru_w_i,
           lru_b_i, lru_lambda, even_w_out, ssm_w_in, ssm_conv_w, ssm_conv_b, ssm_dt_bias, ssm_a_log, ssm_d,
           ssm_norm, ssm_w_out, peer_w_q, peer_sub_keys, peer_u, peer_v):
    b, s, d = x.shape
    t = b * s
    depth = mix_norm.shape[0]
    h = x.reshape(t, d).astype(F32)
    for layer in range(depth):
        j = layer // 2
        if layer % 2 == 0:
            h = _even_layer(h, b, s, mix_norm[layer], even_w_in[j], lru_conv_w[j], lru_conv_b[j], lru_w_a[j],
                            lru_b_a[j], lru_w_i[j], lru_b_i[j], lru_lambda[j], even_w_out[j])
        else:
            h = _odd_layer(h, b, s, mix_norm[layer], ssm_w_in[j], ssm_conv_w[j], ssm_conv_b[j], ssm_dt_bias[j],
                           ssm_a_log[j], ssm_d[j], ssm_norm[j], ssm_w_out[j])
        h = _peer_layer(h, ffn_norm[layer], layer, peer_w_q, peer_sub_keys[layer], peer_u, peer_v,
                        final_norm if layer == depth - 1 else None, tm_norm=_tile(t, 512), tl=_tile(t, 512),
                        tm=_tile(t, 2048), tn=1024, se=256, st=256)
    return h.reshape(b, s, d).astype(x.dtype)
```

```python
import functools
import math

import jax
import jax.numpy as jnp
from jax import lax
from jax.experimental import pallas as pl
from jax.experimental.pallas import tpu as pltpu

F32 = jnp.float32
BF16 = jnp.bfloat16

EPS = 1e-6
LANES = 128
SUBLANES = 8
VMEM_LIMIT = 56 * 1024 * 1024

LRU_BLOCK = 64
LRU_C = 8.0
CONV_WIDTH = 4
RET_HEADS = 4
RET_DIM = 256
CHUNK = 128
ROPE_BASE = 10000.0
SSM_HEAD_DIM = 64
SSM_GROUPS = 4
SSM_STATE = 128
PEER_HEADS = 8
PEER_KEYS = 128
PEER_TOPK = 16
NEG_INF = float("-inf")
POS_INF = float("inf")


def _cparams(*sem):
    return pltpu.CompilerParams(dimension_semantics=sem, vmem_limit_bytes=VMEM_LIMIT)


def _softplus(x):
    return jnp.maximum(x, 0.0) + jnp.log1p(jnp.exp(-jnp.abs(x)))


def _sigmoid(x):
    return 0.5 + 0.5 * jnp.tanh(0.5 * x)


def _silu(x):
    hx = 0.5 * x
    return hx + hx * jnp.tanh(hx)


def _gelu(x):
    c = math.sqrt(2.0 / math.pi)
    hx = 0.5 * x
    return hx + hx * jnp.tanh(x * (c + (c * 0.044715) * (x * x)))


def _pack_rows(x):
    return pltpu.bitcast(x.astype(BF16), jnp.uint32)


def _unpack_rows(x):
    return pltpu.bitcast(x, BF16)


def _rms(x, g):
    return x * lax.rsqrt(jnp.mean(x * x, axis=-1, keepdims=True) + EPS) * g


def _pack_weight_kernel(transpose, w_ref, o_ref):
    w = w_ref[...]
    o_ref[...] = _pack_rows(w.T if transpose else w)


def _pack_weight(w, *, layer=None, transpose=False, rows=512):
    r, c = w.shape[-2:]
    rt = _tile(r, rows)
    if layer is None:
        in_spec = pl.BlockSpec((rt, c), lambda i: (i, 0))
    else:
        in_spec = pl.BlockSpec((None, rt, c), lambda i: (layer, i, 0))
    if transpose:
        out_spec, out_shape = pl.BlockSpec((c // 2, rt), lambda i: (0, i)), (c // 2, r)
    else:
        out_spec, out_shape = pl.BlockSpec((rt // 2, c), lambda i: (i, 0)), (r // 2, c)
    return pl.pallas_call(
        functools.partial(_pack_weight_kernel, transpose),
        grid=(r // rt,),
        in_specs=[in_spec],
        out_specs=out_spec,
        out_shape=jax.ShapeDtypeStruct(out_shape, jnp.uint32),
        compiler_params=_cparams("parallel"),
        name="pack_weight",
    )(w.astype(F32))


def _norm_matmul_kernel(x_ref, g_ref, w_ref, o_ref, hn_ref):
    @pl.when(pl.program_id(1) == 0)
    def _():
        hn_ref[...] = _rms(x_ref[...], g_ref[...]).astype(BF16)

    o_ref[...] = _pack_rows(jnp.dot(hn_ref[...], _unpack_rows(w_ref[...]), preferred_element_type=F32))


def _norm_matmul(x, g, w, *, tm, tn):
    t, d = x.shape
    n = w.shape[1]
    return pl.pallas_call(
        _norm_matmul_kernel,
        grid=(t // tm, n // tn),
        in_specs=[pl.BlockSpec((tm, d), lambda i, j: (i, 0)),
                  pl.BlockSpec((1, d), lambda i, j: (0, 0)),
                  pl.BlockSpec((d // 2, tn), lambda i, j: (0, j))],
        out_specs=pl.BlockSpec((tm // 2, tn), lambda i, j: (i, j)),
        out_shape=jax.ShapeDtypeStruct((t // 2, n), jnp.uint32),
        scratch_shapes=[pltpu.VMEM((tm, d), BF16)],
        compiler_params=_cparams("parallel", "arbitrary"),
        name="norm_matmul",
    )(x, g.reshape(1, d), w)


def _proj_residual_kernel(n_in, x_ref, g_ref, *refs):
    o_ref, t_ref = refs[2 * n_in], refs[2 * n_in + 1]
    acc = x_ref[...]
    for k in range(n_in):
        acc = acc + jnp.dot(_unpack_rows(refs[k][...]), _unpack_rows(refs[n_in + k][...]),
                            preferred_element_type=F32)
    o_ref[...] = acc
    t_ref[...] = _pack_rows(_rms(acc, g_ref[...]).T)


def _proj_residual(x, ys, ws, g_next, *, tm, tn):
    t, d = x.shape
    assert tn == d
    n_in = len(ys)
    in_specs = [pl.BlockSpec((tm, tn), lambda i, j: (i, j)), pl.BlockSpec((1, d), lambda i, j: (0, 0))]
    in_specs += [pl.BlockSpec((tm // 2, y.shape[1]), lambda i, j: (i, 0)) for y in ys]
    in_specs += [pl.BlockSpec((y.shape[1] // 2, tn), functools.partial(lambda rb, i, j: (rb, j), rb))
                 for y, (_, rb) in zip(ys, ws)]
    return pl.pallas_call(
        functools.partial(_proj_residual_kernel, n_in),
        grid=(t // tm, d // tn),
        in_specs=in_specs,
        out_specs=[pl.BlockSpec((tm, tn), lambda i, j: (i, j)), pl.BlockSpec((d // 2, tm), lambda i, j: (0, i))],
        out_shape=[jax.ShapeDtypeStruct((t, d), F32), jax.ShapeDtypeStruct((d // 2, t), jnp.uint32)],
        compiler_params=_cparams("parallel", "parallel"),
        name="proj_residual",
    )(x, g_next.astype(F32).reshape(1, d), *ys, *[w for w, _ in ws])


def _lru_kernel(ga_ref, xa_ref, cw_ref, cb_ref, wg_ref, ba_ref, bi_ref, lam_ref, o_ref, a_s, u_s):
    s, ct = 2 * xa_ref.shape[1], xa_ref.shape[2]
    x = _unpack_rows(xa_ref[0]).astype(F32)
    row = lax.broadcasted_iota(jnp.int32, (s, ct), 0)
    cw = cw_ref[...]
    xc = x * cw[CONV_WIDTH - 1:CONV_WIDTH, :] + cb_ref[...]
    for k in range(1, CONV_WIDTH):
        xs = jnp.where(row >= k, pltpu.roll(x, k, axis=0), 0.0)
        xc = xc + xs * cw[CONV_WIDTH - 1 - k:CONV_WIDTH - k, :]
    gates = jnp.dot(xc.astype(BF16), wg_ref[0], preferred_element_type=F32)
    r = _sigmoid(gates[:, :ct] + ba_ref[...])
    i = _sigmoid(gates[:, ct:] + bi_ref[...])
    log_a = (-LRU_C * r) * _softplus(-lam_ref[...])
    a_s[...] = jnp.exp(log_a)
    th = jnp.tanh(log_a)
    u_s[...] = jnp.sqrt(-2.0 * th / (1.0 - th)) * (i * xc)

    sub = lax.broadcasted_iota(jnp.int32, (SUBLANES, ct), 0)

    def local_scan(r0):
        a = a_s[pl.ds(r0, SUBLANES), :]
        u = u_s[pl.ds(r0, SUBLANES), :]
        for sh in (1, 2, 4):
            keep = sub >= sh
            u = jnp.where(keep, a * pltpu.roll(u, sh, axis=0) + u, u)
            a = jnp.where(keep, a * pltpu.roll(a, sh, axis=0), a)
        return a, u

    def tile_pair(t, h):
        r0 = pl.multiple_of(t * (2 * SUBLANES), 2 * SUBLANES)
        a1, u1 = local_scan(r0)
        a2, u2 = local_scan(r0 + SUBLANES)
        h1 = u1 + a1 * h
        h2 = u2 + a2 * h1[SUBLANES - 1:SUBLANES, :]
        p0 = pl.multiple_of(t * SUBLANES, SUBLANES)
        gate = _gelu(_unpack_rows(ga_ref[0, pl.ds(p0, SUBLANES), :]).astype(F32))
        o_ref[0, pl.ds(p0, SUBLANES), :] = _pack_rows(gate * jnp.concatenate([h1, h2], axis=0))
        return h2[SUBLANES - 1:SUBLANES, :]

    lax.fori_loop(0, s // (2 * SUBLANES), tile_pair, jnp.zeros((1, ct), F32), unroll=2)


def _lru_branch(proj3, conv_w, conv_b, w_gates, b_a, b_i, lam, *, width, ct):
    b, sh, _ = proj3.shape
    s = 2 * sh
    nct = width // ct
    vec = lambda: pl.BlockSpec((1, ct), lambda bi, j: (0, j))
    return pl.pallas_call(
        _lru_kernel,
        grid=(b, nct),
        in_specs=[pl.BlockSpec((1, sh, ct), lambda bi, j: (bi, 0, j)),
                  pl.BlockSpec((1, sh, ct), lambda bi, j: (bi, 0, nct + j)),
                  pl.BlockSpec((CONV_WIDTH, ct), lambda bi, j: (0, j)),
                  vec(),
                  pl.BlockSpec((1, ct, 2 * ct), lambda bi, j: (j, 0, 0)),
                  vec(), vec(), vec()],
        out_specs=pl.BlockSpec((1, sh, ct), lambda bi, j: (bi, 0, j)),
        out_shape=jax.ShapeDtypeStruct((b, sh, width), jnp.uint32),
        scratch_shapes=[pltpu.VMEM((s, ct), F32), pltpu.VMEM((s, ct), F32)],
        compiler_params=_cparams("parallel", "parallel"),
        name="rg_lru",
    )(proj3, proj3, conv_w, conv_b.reshape(1, -1), w_gates, b_a.reshape(1, -1), b_i.reshape(1, -1),
      lam.reshape(1, -1))


def _lru_gate_weights(w_a, w_i, ct):
    nb = w_a.shape[0]
    per = ct // LRU_BLOCK

    def bd(w):
        w = w.reshape(nb // per, per, LRU_BLOCK, LRU_BLOCK)
        eye = jnp.eye(per, dtype=w.dtype)
        return jnp.einsum('tpij,pq->tpiqj', w, eye).reshape(nb // per, ct, ct)

    return jnp.concatenate([bd(w_a), bd(w_i)], axis=-1).astype(BF16)


def _retention_kernel(lg_ref, q_ref, k_ref, v_ref, gb_ref, cos_ref, sin_ref, o_ref, q_s, k_s, kd_s, st_s):
    s, dk = 2 * q_ref.shape[1], q_ref.shape[2]
    half = dk // 2
    lg = lg_ref[pl.program_id(1)]
    cos = cos_ref[...]
    sin = sin_ref[...]

    def rope(x):
        x1, x2 = x[:, :half], x[:, half:]
        return jnp.concatenate([x1 * cos - x2 * sin, x1 * sin + x2 * cos], axis=-1)

    q_s[...] = (rope(_unpack_rows(q_ref[0]).astype(F32)) * (dk ** -0.5)).astype(BF16)
    k_s[...] = rope(_unpack_rows(k_ref[0]).astype(F32))
    st_s[...] = jnp.zeros_like(st_s)

    ri = lax.broadcasted_iota(jnp.int32, (CHUNK, CHUNK), 0)
    ci = lax.broadcasted_iota(jnp.int32, (CHUNK, CHUNK), 1)
    rel = (ri - ci).astype(F32)
    decay_in = jnp.where(rel >= 0, jnp.exp(lg * jnp.maximum(rel, 0.0)), 0.0)
    rowf = lax.broadcasted_iota(jnp.int32, (CHUNK, dk), 0).astype(F32)
    k_decay = jnp.exp(lg * (CHUNK - 1.0 - rowf))
    q_decay = jnp.exp(lg * (rowf + 1.0))
    chunk_decay = jnp.exp(jnp.full((1, dk), lg * CHUNK, F32))

    def chunk(c, carry):
        r0 = pl.multiple_of(c * CHUNK, CHUNK)
        qc = q_s[pl.ds(r0, CHUNK), :]
        kc = k_s[pl.ds(r0, CHUNK), :]
        p0 = pl.multiple_of(c * (CHUNK // 2), CHUNK // 2)
        vc = _unpack_rows(v_ref[0, pl.ds(p0, CHUNK // 2), :])
        scores = lax.dot_general(qc, kc.astype(BF16), (((1,), (1,)), ((), ())), preferred_element_type=F32)
        y = jnp.dot((scores * decay_in).astype(BF16), vc, preferred_element_type=F32)
        state = st_s[...]
        y = y + jnp.dot(qc, state.astype(BF16), preferred_element_type=F32) * q_decay
        kd_s[...] = (kc * k_decay).T.astype(BF16)
        st_s[...] = state * chunk_decay + jnp.dot(kd_s[...], vc, preferred_element_type=F32)
        y = y * lax.rsqrt(jnp.mean(y * y, axis=-1, keepdims=True) + EPS)
        gate = _silu(_unpack_rows(gb_ref[0, pl.ds(p0, CHUNK // 2), :]).astype(F32))
        o_ref[0, pl.ds(p0, CHUNK // 2), :] = _pack_rows(gate * y)
        return carry

    lax.fori_loop(0, s // CHUNK, chunk, 0, unroll=4)


def _retention_branch(proj3, cos, sin, log_gamma, *, col0):
    b, sh, _ = proj3.shape
    s = 2 * sh
    hd = RET_HEADS
    base = col0 // RET_DIM
    blk = lambda off: pl.BlockSpec((1, sh, RET_DIM), lambda bi, h: (bi, 0, base + off + h))
    tab = lambda: pl.BlockSpec((s, RET_DIM // 2), lambda bi, h: (0, 0))
    return pl.pallas_call(
        _retention_kernel,
        grid=(b, hd),
        in_specs=[pl.BlockSpec(memory_space=pltpu.SMEM),
                  blk(0), blk(hd), blk(2 * hd), blk(3 * hd), tab(), tab()],
        out_specs=pl.BlockSpec((1, sh, RET_DIM), lambda bi, h: (bi, 0, h)),
        out_shape=jax.ShapeDtypeStruct((b, sh, hd * RET_DIM), jnp.uint32),
        scratch_shapes=[pltpu.VMEM((s, RET_DIM), BF16), pltpu.VMEM((s, RET_DIM), F32),
                        pltpu.VMEM((RET_DIM, CHUNK), BF16), pltpu.VMEM((RET_DIM, RET_DIM), F32)],
        compiler_params=_cparams("parallel", "parallel"),
        name="retention",
    )(log_gamma, proj3, proj3, proj3, proj3, cos, sin)


def _ssd_kernel(z_ref, xs_ref, bm_ref, cm_ref, dt_ref, cwx_ref, cwb_ref, cwc_ref, cbx_ref, cbb_ref, cbc_ref,
                dtb_ref, aneg_ref, dexp_ref, ng_ref, exp_ref, o_ref, ex_s, eb_s, ec_s, st_s, y_s):
    inner = xs_ref.shape[2]
    gw = inner // SSM_GROUPS
    hpg = gw // SSM_HEAD_DIM

    tail = ex_s.shape[0] - CHUNK

    @pl.when(pl.program_id(1) == 0)
    def _():
        ex_s[0:tail, :] = jnp.zeros((tail, ex_s.shape[1]), BF16)
        eb_s[0:tail, :] = jnp.zeros((tail, eb_s.shape[1]), BF16)
        ec_s[0:tail, :] = jnp.zeros((tail, ec_s.shape[1]), BF16)
        st_s[...] = jnp.zeros_like(st_s)

    sr = lax.broadcasted_iota(jnp.int32, ((CONV_WIDTH - 1) * CHUNK, tail + CHUNK), 0)
    sc = lax.broadcasted_iota(jnp.int32, ((CONV_WIDTH - 1) * CHUNK, tail + CHUNK), 1)
    shift = jnp.where(sc == tail + sr % CHUNK - (sr // CHUNK + 1), 1.0, 0.0).astype(BF16)

    def conv_silu(src_ref, ext, cw_ref, cb_ref, cs):
        x = _unpack_rows(src_ref[0, :, cs])
        ext[tail:tail + CHUNK, cs] = x
        shifted = jnp.dot(shift, ext[:, cs], preferred_element_type=F32)
        cw = cw_ref[:, cs]
        acc = cb_ref[:, cs] + x.astype(F32) * cw[CONV_WIDTH - 1:CONV_WIDTH, :]
        for k in range(1, CONV_WIDTH):
            acc = acc + shifted[(k - 1) * CHUNK:k * CHUNK, :] * cw[CONV_WIDTH - 1 - k:CONV_WIDTH - k, :]
        ext[0:tail, cs] = ext[CHUNK:CHUNK + tail, cs]
        return _silu(acc)

    dt = _softplus(_unpack_rows(dt_ref[0]).astype(F32) + dtb_ref[...])
    a = dt * aneg_ref[...]
    row = lax.broadcasted_iota(jnp.int32, (CHUNK, LANES), 0)
    a_cum = a
    sh = 1
    while sh < CHUNK:
        a_cum = a_cum + jnp.where(row >= sh, pltpu.roll(a_cum, sh, axis=0), 0.0)
        sh *= 2
    a_cum_t = a_cum.T

    def top(w):
        return pltpu.bitcast(pltpu.bitcast(w, jnp.uint32) & jnp.uint32(0xFFFF0000), F32)

    def split3(v):
        hi = top(v)
        mid = top(v - hi)
        return hi.astype(BF16), mid.astype(BF16), (v - hi - mid).astype(BF16)

    def spread(parts, expand):
        return sum(jnp.dot(p, expand, preferred_element_type=F32) for p in parts)

    dt_parts = split3(dt)
    ac_parts = split3(a_cum)

    li = lax.broadcasted_iota(jnp.int32, (CHUNK, CHUNK), 0)
    si = lax.broadcasted_iota(jnp.int32, (CHUNK, CHUNK), 1)
    causal = li >= si
    lane = lax.broadcasted_iota(jnp.int32, (CHUNK, LANES), 1)
    first_half = lane < SSM_HEAD_DIM
    sumsq = jnp.zeros((CHUNK, 1), F32)

    for g in range(SSM_GROUPS):
        cs = slice(g * gw, (g + 1) * gw)
        ns = slice(g * SSM_STATE, (g + 1) * SSM_STATE)
        xs = conv_silu(xs_ref, ex_s, cwx_ref, cbx_ref, cs)
        bg = conv_silu(bm_ref, eb_s, cwb_ref, cbb_ref, ns)
        cg = conv_silu(cm_ref, ec_s, cwc_ref, cbc_ref, ns).astype(BF16)
        expand = exp_ref[:, cs]
        ac_e = spread(ac_parts, expand)
        ac_last = ac_e[CHUNK - 1:CHUNK, :]
        xdt = xs * spread(dt_parts, expand)
        xds = (xdt * jnp.exp(ac_last - ac_e)).astype(BF16)
        xdt_b = xdt.astype(BF16)
        cb = lax.dot_general(cg, bg.astype(BF16), (((1,), (1,)), ((), ())), preferred_element_type=F32)
        state = st_s[g]
        y_off = jnp.dot(cg, state.astype(BF16), preferred_element_type=F32) * jnp.exp(ac_e)
        st_s[g] = state * jnp.exp(ac_last) + jnp.dot(bg.T.astype(BF16), xds, preferred_element_type=F32)
        for pr in range(hpg // 2):
            ps = slice(pr * LANES, (pr + 1) * LANES)
            c0 = g * gw + pr * LANES
            ys = []
            for hh in (2 * pr, 2 * pr + 1):
                hd = g * hpg + hh
                seg = a_cum[:, hd:hd + 1] - a_cum_t[hd:hd + 1, :]
                lmat = jnp.exp(jnp.where(causal, seg, -1e30))
                ys.append(jnp.dot((cb * lmat).astype(BF16), xdt_b[:, ps], preferred_element_type=F32))
            y = jnp.where(first_half, ys[0], ys[1]) + y_off[:, ps] + dexp_ref[:, c0:c0 + LANES] * xs[:, ps]
            y = y * _silu(_unpack_rows(z_ref[0, :, c0:c0 + LANES]).astype(F32))
            y_s[:, c0:c0 + LANES] = y
            sumsq = sumsq + jnp.sum(y * y, axis=-1, keepdims=True)

    o_ref[0] = _pack_rows(y_s[...] * lax.rsqrt(sumsq * (1.0 / inner) + EPS) * ng_ref[...])


def _ssd_core(proj3, conv_w, conv_b, dt_bias, a_log, d_skip, norm_g, *, inner):
    b, sh, _ = proj3.shape
    ch = CHUNK // 2
    gn = SSM_GROUPS * SSM_STATE
    heads = inner // SSM_HEAD_DIM
    pad = LANES - heads
    dtb = jnp.pad(dt_bias.astype(F32), (0, pad)).reshape(1, LANES)
    aneg = jnp.pad(-jnp.exp(a_log.astype(F32)), (0, pad)).reshape(1, LANES)
    dexp = jnp.repeat(d_skip.astype(F32), SSM_HEAD_DIM).reshape(1, inner)
    expand = (jnp.arange(LANES)[:, None] == (jnp.arange(inner) // SSM_HEAD_DIM)[None, :]).astype(BF16)
    cw = conv_w.astype(F32)
    cbias = conv_b.astype(F32).reshape(1, -1)
    full = lambda shape: pl.BlockSpec(shape, lambda bi, c: tuple(0 for _ in shape))
    return pl.pallas_call(
        _ssd_kernel,
        grid=(b, sh // ch),
        in_specs=[pl.BlockSpec((1, ch, inner), lambda bi, c: (bi, c, 0)),
                  pl.BlockSpec((1, ch, inner), lambda bi, c: (bi, c, 1)),
                  pl.BlockSpec((1, ch, gn), lambda bi, c: (bi, c, 2 * inner // gn)),
                  pl.BlockSpec((1, ch, gn), lambda bi, c: (bi, c, 2 * inner // gn + 1)),
                  pl.BlockSpec((1, ch, LANES), lambda bi, c: (bi, c, (2 * inner + 2 * gn) // LANES)),
                  full((CONV_WIDTH, inner)), full((CONV_WIDTH, gn)), full((CONV_WIDTH, gn)),
                  full((1, inner)), full((1, gn)), full((1, gn)),
                  full((1, LANES)), full((1, LANES)), full((1, inner)), full((1, inner)),
                  full((LANES, inner))],
        out_specs=pl.BlockSpec((1, ch, inner), lambda bi, c: (bi, c, 0)),
        out_shape=jax.ShapeDtypeStruct((b, sh, inner), jnp.uint32),
        scratch_shapes=[pltpu.VMEM((CHUNK + 2 * SUBLANES, inner), BF16),
                        pltpu.VMEM((CHUNK + 2 * SUBLANES, gn), BF16),
                        pltpu.VMEM((CHUNK + 2 * SUBLANES, gn), BF16),
                        pltpu.VMEM((SSM_GROUPS, SSM_STATE, inner // SSM_GROUPS), F32),
                        pltpu.VMEM((CHUNK, inner), F32)],
        compiler_params=_cparams("parallel", "arbitrary"),
        name="ssd",
    )(proj3, proj3, proj3, proj3, proj3,
      cw[:, :inner], cw[:, inner:inner + gn], cw[:, inner + gn:],
      cbias[:, :inner], cbias[:, inner:inner + gn], cbias[:, inner + gn:],
      dtb, aneg, dexp, norm_g.astype(F32).reshape(1, inner), expand)


def _sort16_pairs():
    n, pairs, p = PEER_TOPK, [], 1
    while p < n:
        k = p
        while k >= 1:
            for j in range(k % p, n - k, 2 * k):
                for i in range(min(k, n - j - k)):
                    if (i + j) // (2 * p) == (i + j + k) // (2 * p):
                        pairs.append((i + j, i + j + k))
            k //= 2
        p *= 2
    return pairs


def _top_values(s):
    n = PEER_TOPK
    v = [s[i * SUBLANES:(i + 1) * SUBLANES, :] for i in range(n)]

    def exchange(x, i, j):
        x[i], x[j] = jnp.maximum(x[i], x[j]), jnp.minimum(x[i], x[j])

    for i, j in _sort16_pairs():
        exchange(v, i, j)
    shift = SUBLANES // 2
    while shift >= 1:
        v = [jnp.maximum(v[i], pltpu.roll(v[n - 1 - i], shift, axis=0)) for i in range(n)]
        j = n // 2
        while j >= 1:
            for i in range(n):
                if i ^ j > i:
                    exchange(v, i, i ^ j)
            j //= 2
        shift //= 2
    return v


def _dup_bf16(x):
    bits = pltpu.bitcast(x, jnp.uint32)
    hi = (bits + jnp.uint32(0x7FFF) + ((bits >> 16) & jnp.uint32(1))) & jnp.uint32(0xFFFF0000)
    return hi | (hi >> 16)


def _peer_select_kernel(hnt_ref, wqt_ref, keys_ref, c1_ref, w1_ref, r2_ref, w2_ref, qt_s):
    tl = hnt_ref.shape[1]
    kd = keys_ref.shape[2]
    qt_s[...] = jnp.dot(_unpack_rows(wqt_ref[...]), _unpack_rows(hnt_ref[...]),
                        preferred_element_type=F32)
    rk = lax.broadcasted_iota(jnp.int32, (PEER_TOPK, LANES), 0).astype(F32)

    def head(h, carry):
        for c in range(tl // LANES):
            ls = slice(c * LANES, (c + 1) * LANES)
            r1 = pl.multiple_of(h * (2 * kd), 2 * kd)
            q1 = qt_s[pl.ds(r1, kd), ls].astype(BF16)
            q2 = qt_s[pl.ds(r1 + kd, kd), ls].astype(BF16)
            s1 = jnp.dot(keys_ref[2 * h], q1, preferred_element_type=F32)
            s2 = jnp.dot(keys_ref[2 * h + 1], q2, preferred_element_type=F32)
            a_t = _top_values(s1)
            b_t = _top_values(s2)
            a = jnp.concatenate([t[0:1] for t in a_t], axis=0)
            b = jnp.concatenate([t[0:1] for t in b_t], axis=0)
            rep = s1.shape[0] // SUBLANES
            rank2 = jnp.zeros(s2.shape, F32)
            for r in range(PEER_TOPK):
                rank2 = jnp.where(jnp.tile(b_t[r], (rep, 1)) > s2, float(r + 1), rank2)
            a0, b0 = a[0:1], b[0:1]
            best0 = a0 + b0
            cnt = jnp.zeros((PEER_TOPK, LANES), F32)
            cur = a + b0
            z = jnp.zeros((1, LANES), F32)
            for _ in range(PEER_TOPK):
                m = jnp.max(cur, axis=0, keepdims=True)
                idx = jnp.min(jnp.where(cur == m, rk, float(PEER_TOPK)), axis=0, keepdims=True)
                sel = rk == idx
                z = z + jnp.exp(m - best0)
                cnt = jnp.where(sel, cnt + 1.0, cnt)
                csel = jnp.max(jnp.where(sel, cnt, 0.0), axis=0, keepdims=True)
                nb = jnp.max(jnp.where(rk == csel, b, NEG_INF), axis=0, keepdims=True)
                cur = jnp.where(sel, a + nb, cur)
            cnt1 = jnp.zeros(s1.shape, F32)
            for r in range(PEER_TOPK):
                cnt1 = jnp.where(s1 == jnp.tile(a_t[r], (rep, 1)), cnt[r:r + 1], cnt1)
            c1_ref[h, :, ls] = _dup_bf16(cnt1)
            w1_ref[h, :, ls] = _dup_bf16(jnp.exp(s1 - a0))
            r2_ref[h, :, ls] = _pack_rows(rank2)
            w2_ref[h, :, ls] = _pack_rows(jnp.exp(s2 - b0) * (1.0 / z))
        return carry

    lax.fori_loop(0, PEER_HEADS, head, 0, unroll=4)


def _peer_select(hnt, wqt, keys, *, tl):
    dh, t = hnt.shape
    nqh = wqt.shape[0]
    nk = keys.shape[1]
    dup = jax.ShapeDtypeStruct((PEER_HEADS, nk, t), jnp.uint32)
    half = jax.ShapeDtypeStruct((PEER_HEADS, nk // 2, t), jnp.uint32)
    ospec = lambda: pl.BlockSpec((PEER_HEADS, nk, tl), lambda i: (0, 0, i))
    hspec = lambda: pl.BlockSpec((PEER_HEADS, nk // 2, tl), lambda i: (0, 0, i))
    return pl.pallas_call(
        _peer_select_kernel,
        grid=(t // tl,),
        in_specs=[pl.BlockSpec((dh, tl), lambda i: (0, i)),
                  pl.BlockSpec(wqt.shape, lambda i: (0, 0)),
                  pl.BlockSpec(keys.shape, lambda i: (0, 0, 0))],
        out_specs=[ospec(), ospec(), hspec(), hspec()],
        out_shape=[dup, dup, half, half],
        scratch_shapes=[pltpu.VMEM((2 * nqh, tl), F32)],
        compiler_params=_cparams("parallel"),
        name="peer_select",
    )(hnt, wqt, keys)


def _peer_dense_kernel(se, st, final, hnt_ref, u_ref, vt_ref, c1_ref, w1_ref, r2_ref, w2_ref, x_ref, zero_ref, fg_ref, o_ref, acc_s, ht0, ht1, act0, act1, g0, g1):
    ht_b, act_b, g_b = (ht0, ht1), (act0, act1), (g0, g1)
    tn = 2 * u_ref.shape[0]
    tm = hnt_ref.shape[1]
    j = pl.program_id(1)

    @pl.when(j == 0)
    def _():
        acc_s[...] = jnp.zeros_like(acc_s)

    per = se // PEER_KEYS
    zero = jnp.zeros((PEER_KEYS, LANES), BF16)
    blocks = [(c2, k) for c2 in range(tm // st) for k in range(tn // se)]

    def row_bf16(ref, h, i1, ls):
        words = jnp.broadcast_to(ref[h, i1:i1 + 1, ls], (PEER_KEYS // 2, LANES))
        return pltpu.bitcast(words, BF16)

    deps = {}

    def hidden(n):
        c2, k = blocks[n]
        u = _unpack_rows(u_ref[k * se // 2:(k + 1) * se // 2, :])
        hw = hnt_ref[:, c2 * st:(c2 + 1) * st]
        if n - 1 in deps:
            gate0 = hw[0:LANES, :] | jnp.tile(deps.pop(n - 1), (LANES // SUBLANES, st // LANES))
            hw = jnp.concatenate([gate0, hw[LANES:, :]], axis=0)
        ht_b[n % 2][...] = jnp.dot(u, _unpack_rows(hw), preferred_element_type=F32)

    def gates(n):
        c2, k = blocks[n]
        live = None
        for cc in range(st // LANES):
            ls = slice(c2 * st + cc * LANES, c2 * st + (cc + 1) * LANES)
            gs = [None] * per
            for h in range(PEER_HEADS):
                r2 = _unpack_rows(r2_ref[h, :, ls])
                w2 = _unpack_rows(w2_ref[h, :, ls])
                for il in range(per):
                    i1 = k * per + il
                    term = jnp.where(r2 < row_bf16(c1_ref, h, i1, ls), w2, zero) * row_bf16(w1_ref, h, i1, ls)
                    gs[il] = term if gs[il] is None else gs[il] + term
            for il in range(per):
                g_b[n % 2][il * PEER_KEYS:(il + 1) * PEER_KEYS, cc * LANES:(cc + 1) * LANES] = gs[il]
                bits = pltpu.bitcast(gs[il], jnp.uint32)
                for r in range(bits.shape[0] // SUBLANES):
                    piece = bits[r * SUBLANES:(r + 1) * SUBLANES, :]
                    live = piece if live is None else live | piece
        deps[n] = live & zero_ref[...]

    def activate(n):
        act_b[n % 2][...] = _gelu(ht_b[n % 2][...]).astype(BF16) * g_b[n % 2][...]

    def project(n):
        c2, k = blocks[n]
        vt = _unpack_rows(vt_ref[:, k * se:(k + 1) * se])
        acc_s[:, c2 * st:(c2 + 1) * st] += jnp.dot(vt, act_b[n % 2][...], preferred_element_type=F32)

    hidden(0)
    gates(0)
    for n in range(len(blocks)):
        if n + 1 < len(blocks):
            hidden(n + 1)
            gates(n + 1)
        activate(n)
        if n >= 1:
            project(n - 1)
    project(len(blocks) - 1)

    @pl.when(j == pl.num_programs(1) - 1)
    def _():
        y = x_ref[...] + acc_s[...].T
        o_ref[...] = _rms(y, fg_ref[...]) if final else y


def _peer_dense(x, hnt, u, vt, c1, w1, r2, w2, final_g, *, tm, tn, se, st):
    t, d = x.shape
    e = 2 * u.shape[0]
    nk = c1.shape[1]
    rows = tn // PEER_KEYS
    return pl.pallas_call(
        functools.partial(_peer_dense_kernel, se, st, final_g is not None),
        grid=(t // tm, e // tn),
        in_specs=[pl.BlockSpec((d // 2, tm), lambda i, j: (0, i), pipeline_mode=pl.Buffered(1)),
                  pl.BlockSpec((tn // 2, d), lambda i, j: (j, 0)),
                  pl.BlockSpec((d // 2, tn), lambda i, j: (0, j)),
                  pl.BlockSpec((PEER_HEADS, rows, tm), lambda i, j: (0, j, i)),
                  pl.BlockSpec((PEER_HEADS, rows, tm), lambda i, j: (0, j, i)),
                  pl.BlockSpec((PEER_HEADS, nk // 2, tm), lambda i, j: (0, 0, i), pipeline_mode=pl.Buffered(1)),
                  pl.BlockSpec((PEER_HEADS, nk // 2, tm), lambda i, j: (0, 0, i), pipeline_mode=pl.Buffered(1)),
                  pl.BlockSpec((tm, d), lambda i, j: (i, 0), pipeline_mode=pl.Buffered(1)),
                  pl.BlockSpec((SUBLANES, LANES), lambda i, j: (0, 0)),
                  pl.BlockSpec((1, d), lambda i, j: (0, 0))],
        out_specs=pl.BlockSpec((tm, d), lambda i, j: (i, 0), pipeline_mode=pl.Buffered(1)),
        out_shape=jax.ShapeDtypeStruct((t, d), F32),
        scratch_shapes=[pltpu.VMEM((d, tm), F32), pltpu.VMEM((se, st), F32), pltpu.VMEM((se, st), F32),
                        pltpu.VMEM((se, st), BF16), pltpu.VMEM((se, st), BF16),
                        pltpu.VMEM((se, st), BF16), pltpu.VMEM((se, st), BF16)],
        compiler_params=_cparams("parallel", "arbitrary"),
        name="peer_dense",
    )(hnt, u, vt, c1, w1, r2, w2, x, jnp.zeros((SUBLANES, LANES), jnp.uint32),
      (jnp.ones((d,), F32) if final_g is None else final_g.astype(F32)).reshape(1, d))


def _peer_layer(x, hnt, layer, w_q, sub_keys, u_emb, v_emb, final_g, *, tl, tm, tn, se, st):
    hd, two, nk, kd = sub_keys.shape
    c1, w1, r2, w2 = _peer_select(hnt, _pack_weight(w_q, layer=layer, transpose=True),
                                  sub_keys.reshape(hd * two, nk, kd).astype(BF16), tl=tl)
    return _peer_dense(x, hnt, _pack_weight(u_emb, layer=layer), _pack_weight(v_emb, layer=layer, transpose=True),
                       c1, w1, r2, w2, final_g, tm=tm, tn=tn, se=se, st=st)


def _rope_tables(s, dk):
    half = dk // 2
    inv = ROPE_BASE ** (-jnp.arange(half, dtype=F32) * (2.0 / dk))
    ang = jnp.arange(s, dtype=F32)[:, None] * inv[None, :]
    return jnp.cos(ang), jnp.sin(ang)


def _tile(n, want):
    if n <= want:
        return n
    for cand in range(want, 0, -LANES):
        if n % cand == 0:
            return cand
    return n


def _even_layer(x, b, s, g, g_next, w_in, conv_w, conv_b, w_a, b_a, w_i, b_i, lam, w_out):
    t, d = x.shape
    width = w_a.shape[0] * LRU_BLOCK
    n_in = w_in.shape[1]
    proj = _norm_matmul(x, g, _pack_weight(w_in), tm=_tile(t, 1024), tn=_tile(n_in, n_in // 2))
    proj3 = proj.reshape(b, s // 2, n_in)
    ct = 256
    ya = _lru_branch(proj3, conv_w.astype(F32), conv_b.astype(F32), _lru_gate_weights(w_a, w_i, ct),
                     b_a.astype(F32), b_i.astype(F32), lam.astype(F32), width=width, ct=ct)
    cos, sin = _rope_tables(s, RET_DIM)
    log_gamma = jnp.log1p(-jnp.exp2(-5.0 - jnp.arange(RET_HEADS, dtype=F32)))
    yb = _retention_branch(proj3, cos, sin, log_gamma, col0=2 * width)
    wo = _pack_weight(w_out)
    return _proj_residual(x, [ya.reshape(t // 2, width), yb.reshape(t // 2, -1)], [(wo, 0), (wo, 1)],
                          g_next, tm=_tile(t, 1024), tn=d)


def _odd_layer(x, b, s, g, g_next, w_in, conv_w, conv_b, dt_bias, a_log, d_skip, norm_g, w_out):
    t, d = x.shape
    inner = w_out.shape[0]
    n_in = w_in.shape[1]
    n_pad = -(-n_in // (2 * LANES)) * (2 * LANES)
    tn = n_pad // 2
    w_in_p = _pack_weight(jnp.pad(w_in.astype(F32), ((0, 0), (0, n_pad - n_in))))
    proj = _norm_matmul(x, g, w_in_p, tm=_tile(t, 1024), tn=tn)
    y = _ssd_core(proj.reshape(b, s // 2, n_pad), conv_w, conv_b, dt_bias, a_log, d_skip, norm_g, inner=inner)
    return _proj_residual(x, [y.reshape(t // 2, inner)], [(_pack_weight(w_out), 0)], g_next,
                          tm=_tile(t, 1024), tn=d)


def kernel(x, mix_norm, ffn_norm, final_norm, even_w_in, lru_conv_w, lru_conv_b, lru_w_a, lru_b_a, lru_w_i,
           lru_b_i, lru_lambda, even_w_out, ssm_w_in, ssm_conv_w, ssm_conv_b, ssm_dt_bias, ssm_a_log, ssm_d,
           ssm_norm, ssm_w_out, peer_w_q, peer_sub_keys, peer_u, peer_v):
    b, s, d = x.shape
    t = b * s
    depth = mix_norm.shape[0]
    h = x.reshape(t, d).astype(F32)
    for layer in range(depth):
        j = layer // 2
        if layer % 2 == 0:
            h, hnt = _even_layer(h, b, s, mix_norm[layer], ffn_norm[layer], even_w_in[j], lru_conv_w[j], lru_conv_b[j], lru_w_a[j],
                            lru_b_a[j], lru_w_i[j], lru_b_i[j], lru_lambda[j], even_w_out[j])
        else:
            h, hnt = _odd_layer(h, b, s, mix_norm[layer], ffn_norm[layer], ssm_w_in[j], ssm_conv_w[j], ssm_conv_b[j], ssm_dt_bias[j],
                           ssm_a_log[j], ssm_d[j], ssm_norm[j], ssm_w_out[j])
        h = _peer_layer(h, hnt, layer, peer_w_q, peer_sub_keys[layer], peer_u, peer_v,
                        final_norm if layer == depth - 1 else None, tl=_tile(t, 512),
                        tm=_tile(t, 2048), tn=1024, se=256, st=256)
    return h.reshape(b, s, d).astype(x.dtype)
```
